```python
import jax
import jax.numpy as jnp
from jax import lax
import numpy as np

D_MODEL = 4096
BATCH = 1
SEQ = 16384
DEPTH = 4

GRID_W = 64
CTX_LEN = 256
N_BRANCH = 4
BRANCH_W = D_MODEL // N_BRANCH

HEAD_DIM = 128
N_Q_HEADS = BRANCH_W // HEAD_DIM
N_KV_HEADS = 2
GROUP = N_Q_HEADS // N_KV_HEADS
WINDOW = 128
ATT_BLOCK = 128
ROPE_THETA = 10000.0
NEG_INF = -1e30

LRU_W = BRANCH_W
LRU_BLOCKS = 8
LRU_C = 8.0
LRU_CONV = 4
LRU_PAD = ((LRU_CONV - 1) // 2, LRU_CONV // 2)

FOURIER_W = BRANCH_W
FOURIER_GROUPS = 4

CONF_W = BRANCH_W
CONF_KERNEL = 31
CONF_PAD = ((CONF_KERNEL - 1) // 2, (CONF_KERNEL - 1) // 2)

GATE_BLOCKS = 16
FFN_DENSE = 5120
N_EXPERTS = 8
TOP_K = 2
FFN_EXPERT = FFN_DENSE // N_EXPERTS
EPS = 1e-6

IN_SIZES = (N_Q_HEADS * HEAD_DIM, N_KV_HEADS * HEAD_DIM, N_KV_HEADS * HEAD_DIM,
            LRU_W, LRU_W, FOURIER_W, 2 * CONF_W)
IN_W = sum(IN_SIZES)
IN_OFFSETS = tuple(int(o) for o in np.cumsum(IN_SIZES)[:-1])

kernel_name = 'hybrid_diffusion_parallel_mixer_block'

F32 = jnp.float32


def rms_norm(x, g):
    xf = x.astype(F32)
    y = xf * lax.rsqrt(jnp.mean(xf * xf, axis=-1, keepdims=True) + EPS)
    return (y * g.astype(F32)).astype(x.dtype)


def layer_norm(x, g, b):
    xf = x.astype(F32)
    mu = jnp.mean(xf, axis=-1, keepdims=True)
    d = xf - mu
    var = jnp.mean(d * d, axis=-1, keepdims=True)
    return (d * lax.rsqrt(var + EPS) * g.astype(F32) + b.astype(F32)).astype(x.dtype)


def modulate(h, shift, scale):
    return h * (1 + scale) + shift


def split_heads(t, n_heads):
    return t.reshape(t.shape[0], t.shape[1], n_heads, HEAD_DIM)


def axial_rope_tables(length):
    rows = length // GRID_W
    row = jnp.broadcast_to(jnp.arange(rows, dtype=F32)[:, None], (rows, GRID_W)).reshape(-1)
    col = jnp.broadcast_to(jnp.arange(GRID_W, dtype=F32)[None, :], (rows, GRID_W)).reshape(-1)
    n_freq = HEAD_DIM // 4
    inv_freq = ROPE_THETA ** (-jnp.arange(n_freq, dtype=F32) / n_freq)
    ang_r = row[:, None] * inv_freq
    ang_c = col[:, None] * inv_freq
    return (jnp.cos(ang_r)[:, None, :], jnp.sin(ang_r)[:, None, :],
            jnp.cos(ang_c)[:, None, :], jnp.sin(ang_c)[:, None, :])


def rope_rotate(x, cos, sin):
    x1, x2 = jnp.split(x, 2, axis=-1)
    return jnp.concatenate([x1 * cos - x2 * sin, x2 * cos + x1 * sin], axis=-1)


def apply_axial_rope(x, rope):
    cos_r, sin_r, cos_c, sin_c = rope
    xr, xc = jnp.split(x.astype(F32), 2, axis=-1)
    return jnp.concatenate([rope_rotate(xr, cos_r, sin_r), rope_rotate(xc, cos_c, sin_c)], axis=-1).astype(x.dtype)


def softmax_with_sink(s, sink):
    m = jnp.maximum(jnp.max(s, axis=-1, keepdims=True), sink)
    e = jnp.exp(s - m)
    return e / (jnp.sum(e, axis=-1, keepdims=True) + jnp.exp(sink - m))


def windowed_attention(q, k, v, k_ctx, v_ctx, sink):
    B, L = q.shape[:2]
    nb = L // ATT_BLOCK
    scale = HEAD_DIM ** -0.5
    qb = q.reshape(B, nb, ATT_BLOCK, N_KV_HEADS, GROUP, HEAD_DIM)

    def neighbours(t):
        tp = jnp.pad(t, ((0, 0), (ATT_BLOCK, ATT_BLOCK), (0, 0), (0, 0)))
        tp = tp.reshape(B, nb + 2, ATT_BLOCK, N_KV_HEADS, HEAD_DIM)
        return jnp.concatenate([tp[:, :-2], tp[:, 1:-1], tp[:, 2:]], axis=2)

    kb, vb = neighbours(k), neighbours(v)
    s_loc = jnp.einsum('bnqkgd,bnskd->bnkgqs', qb, kb, preferred_element_type=F32) * scale
    s_ctx = jnp.einsum('bnqkgd,bskd->bnkgqs', qb, k_ctx, preferred_element_type=F32) * scale
    q_idx = jnp.arange(nb)[:, None, None] * ATT_BLOCK + jnp.arange(ATT_BLOCK)[None, :, None]
    k_idx = (jnp.arange(nb)[:, None, None] - 1) * ATT_BLOCK + jnp.arange(3 * ATT_BLOCK)[None, None, :]
    valid = (jnp.abs(k_idx - q_idx) <= WINDOW) & (k_idx >= 0) & (k_idx < L)
    s_loc = jnp.where(valid[None, :, None, None], s_loc, NEG_INF)
    p = softmax_with_sink(jnp.concatenate([s_loc, s_ctx], axis=-1), sink[None, None, :, :, None, None])
    p_loc = p[..., :3 * ATT_BLOCK].astype(v.dtype)
    p_ctx = p[..., 3 * ATT_BLOCK:].astype(v.dtype)
    o = (jnp.einsum('bnkgqs,bnskd->bnqkgd', p_loc, vb)
         + jnp.einsum('bnkgqs,bskd->bnqkgd', p_ctx, v_ctx))
    return o.reshape(B, L, N_Q_HEADS * HEAD_DIM)


def context_attention(q, k, v, sink):
    B, C = q.shape[:2]
    s = jnp.einsum('bqkgd,bskd->bkgqs', q, k, preferred_element_type=F32) * HEAD_DIM ** -0.5
    p = softmax_with_sink(s, sink[None, :, :, None, None]).astype(v.dtype)
    return jnp.einsum('bkgqs,bskd->bqkgd', p, v).reshape(B, C, N_Q_HEADS * HEAD_DIM)


def depthwise_conv(x, w, b, pad):
    y = lax.conv_general_dilated(x, w[:, None, :].astype(x.dtype), window_strides=(1,), padding=(pad,),
                                 dimension_numbers=('NWC', 'WIO', 'NWC'), feature_group_count=x.shape[-1])
    return y + b.astype(x.dtype)


def block_diag(x, w, b):
    B, L, W = x.shape
    n = w.shape[0]
    y = jnp.einsum('blnc,ncd->blnd', x.reshape(B, L, n, W // n), w)
    return y.reshape(B, L, W) + b


def rglru_coeffs(u, lam, wa, ba, wx, bx):
    uf = u.astype(F32)
    r = jax.nn.sigmoid(block_diag(uf, wa.astype(F32), ba.astype(F32)))
    i = jax.nn.sigmoid(block_diag(uf, wx.astype(F32), bx.astype(F32)))
    log_a = -LRU_C * r * jax.nn.softplus(-lam.astype(F32))
    b = jnp.sqrt(-jnp.expm1(2.0 * log_a)) * (i * uf)
    return jnp.exp(log_a), b


def linear_scan(a, b, reverse):
    def combine(left, right):
        a_l, b_l = left
        a_r, b_r = right
        return a_l * a_r, a_r * b_l + b_r
    return lax.associative_scan(combine, (a, b), reverse=reverse, axis=1)


def recurrent_branch(u_l, g_l, u_c, g_c, conv_w, conv_b, lam, wa, ba, wx, bx):
    x_l = depthwise_conv(u_l, conv_w, conv_b, LRU_PAD)
    x_c = depthwise_conv(u_c, conv_w, conv_b, LRU_PAD)
    hs_l, hs_c = [], []
    for d, reverse in enumerate((False, True)):
        a_c, b_c = rglru_coeffs(x_c, lam[d], wa[d], ba[d], wx[d], bx[d])
        _, h_c = linear_scan(a_c, b_c, reverse)
        h_final = h_c[:, 0] if reverse else h_c[:, -1]
        a_l, b_l = rglru_coeffs(x_l, lam[d], wa[d], ba[d], wx[d], bx[d])
        a_cum, b_cum = linear_scan(a_l, b_l, reverse)
        hs_l.append(a_cum * h_final[:, None, :] + b_cum)
        hs_c.append(h_c)
    y_l = (hs_l[0] + hs_l[1]).astype(u_l.dtype) * jax.nn.gelu(g_l)
    y_c = None if g_c is None else (hs_c[0] + hs_c[1]).astype(u_c.dtype) * jax.nn.gelu(g_c)
    return y_l, y_c


def fourier_branch(u):
    B, L, _ = u.shape
    ug = u.astype(F32).reshape(B, L, FOURIER_GROUPS, FOURIER_W // FOURIER_GROUPS)
    y = jnp.fft.fft2(ug, axes=(1, 3), norm='ortho').real
    return y.reshape(B, L, FOURIER_W).astype(u.dtype)


def conformer_branch(u, conv_w, conv_b, ln_g, ln_b):
    a, g = jnp.split(u, 2, axis=-1)
    v = depthwise_conv(a * jax.nn.sigmoid(g), conv_w, conv_b, CONF_PAD)
    return jax.nn.silu(layer_norm(v, ln_g, ln_b))


def merge_branches(h, ys, w_branch, gate_w, gate_b, w_out):
    B, L, D = h.shape
    hb = h.reshape(B, L, GATE_BLOCKS, D // GATE_BLOCKS)
    acc = None
    for bi, y in enumerate(ys):
        gate = jax.nn.sigmoid(jnp.einsum('blnc,ncd->blnd', hb, gate_w[bi]).reshape(B, L, D) + gate_b[bi])
        term = gate * (y @ w_branch[bi])
        acc = term if acc is None else acc + term
    return acc @ w_out


def token_mixer(h_l, h_c, rope, need_ctx, w_in, qk_norm_g, attn_sink, lru_conv_w, lru_conv_b, lru_lambda,
                lru_wa, lru_ba, lru_wx, lru_bx, conf_conv_w, conf_conv_b, conf_ln_g, conf_ln_b,
                w_branch, gate_w, gate_b, w_out):
    q_l, k_l, v_l, u_l, g_l, f_l, c_l = jnp.split(h_l @ w_in, IN_OFFSETS, axis=-1)
    if need_ctx:
        q_c, k_c, v_c, u_c, g_c, f_c, c_c = jnp.split(h_c @ w_in, IN_OFFSETS, axis=-1)
    else:
        w_k, w_v, w_u = jnp.split(w_in, IN_OFFSETS, axis=-1)[1:4]
        k_c, v_c, u_c, g_c = h_c @ w_k, h_c @ w_v, h_c @ w_u, None
    qn_g, kn_g = qk_norm_g[0], qk_norm_g[1]
    sink = attn_sink.astype(F32).reshape(N_KV_HEADS, GROUP)
    B, L = h_l.shape[:2]
    q_l = apply_axial_rope(rms_norm(split_heads(q_l, N_Q_HEADS), qn_g), rope)
    k_l = apply_axial_rope(rms_norm(split_heads(k_l, N_KV_HEADS), kn_g), rope)
    v_l = split_heads(v_l, N_KV_HEADS)
    k_c = rms_norm(split_heads(k_c, N_KV_HEADS), kn_g)
    v_c = split_heads(v_c, N_KV_HEADS)
    ya_l = windowed_attention(q_l.reshape(B, L, N_KV_HEADS, GROUP, HEAD_DIM), k_l, v_l, k_c, v_c, sink)
    yb_l, yb_c = recurrent_branch(u_l, g_l, u_c, g_c, lru_conv_w, lru_conv_b, lru_lambda,
                                  lru_wa, lru_ba, lru_wx, lru_bx)
    yc_l = fourier_branch(f_l)
    yd_l = conformer_branch(c_l, conf_conv_w, conf_conv_b, conf_ln_g, conf_ln_b)
    out_l = merge_branches(h_l, (ya_l, yb_l, yc_l, yd_l), w_branch, gate_w, gate_b, w_out)
    if not need_ctx:
        return out_l, None
    C = h_c.shape[1]
    q_c = rms_norm(split_heads(q_c, N_Q_HEADS), qn_g).reshape(B, C, N_KV_HEADS, GROUP, HEAD_DIM)
    ya_c = context_attention(q_c, k_c, v_c, sink)
    yc_c = fourier_branch(f_c)
    yd_c = conformer_branch(c_c, conf_conv_w, conf_conv_b, conf_ln_g, conf_ln_b)
    out_c = merge_branches(h_c, (ya_c, yb_c, yc_c, yd_c), w_branch, gate_w, gate_b, w_out)
    return out_l, out_c


def swiglu(h, w_gate, w_up, w_down):
    return (jax.nn.silu(h @ w_gate) * (h @ w_up)) @ w_down


def moe_swiglu(h, router_w, router_b, w_gate, w_up, w_down):
    logits = jnp.einsum('bld,de->ble', h, router_w, preferred_element_type=F32) + router_b.astype(F32)
    top_val, top_idx = lax.top_k(logits, TOP_K)
    top_w = jax.nn.softmax(top_val, axis=-1)
    gates = jnp.sum(jax.nn.one_hot(top_idx, N_EXPERTS, dtype=F32) * top_w[..., None], axis=-2).astype(h.dtype)
    hid = jax.nn.silu(jnp.einsum('bld,edf->blef', h, w_gate)) * jnp.einsum('bld,edf->blef', h, w_up)
    return jnp.einsum('blef,efd->bld', hid * gates[..., None], w_down)


def channel_mixer(h, layer, ffn_w_gate, ffn_w_up, ffn_w_down, router_w, router_b, moe_w_gate, moe_w_up, moe_w_down):
    i = layer // 2
    if layer % 2 == 0:
        return swiglu(h, ffn_w_gate[i], ffn_w_up[i], ffn_w_down[i])
    return moe_swiglu(h, router_w[i], router_b[i], moe_w_gate[i], moe_w_up[i], moe_w_down[i])


def setup_inputs(seed: int = 0) -> dict:
    key = jax.random.key(seed)
    ks = iter(jax.random.split(key, 48))
    n_l, n_d, n_m = DEPTH, (DEPTH + 1) // 2, DEPTH // 2
    D = D_MODEL
    bw = LRU_W // LRU_BLOCKS
    gw = D // GATE_BLOCKS

    def nrm(shape, scale):
        return jax.random.normal(next(ks), shape, F32) * scale

    x = nrm((BATCH, SEQ, D), 1.0)
    c = nrm((BATCH, D), 1.0)
    ctx = nrm((BATCH, CTX_LEN, D), 1.0)
    c_ctx = nrm((D,), 1.0)
    ada_w = nrm((n_l, D, 6 * D), 0.5 * D ** -0.5)
    ada_b = nrm((n_l, 6 * D), 0.01)
    norm_mix_g = 1.0 + nrm((n_l, D), 0.02)
    norm_ffn_g = 1.0 + nrm((n_l, D), 0.02)
    w_in = nrm((n_l, D, IN_W), D ** -0.5)
    qk_norm_g = 1.0 + nrm((n_l, 2, HEAD_DIM), 0.02)
    attn_sink = nrm((n_l, N_Q_HEADS), 1.0)
    lru_conv_w = nrm((n_l, LRU_CONV, LRU_W), LRU_CONV ** -0.5)
    lru_conv_b = nrm((n_l, LRU_W), 0.01)
    a_pow = jax.random.uniform(next(ks), (n_l, 2, LRU_W), F32, 0.9, 0.999)
    a_base = a_pow ** (1.0 / LRU_C)
    lru_lambda = jnp.log(a_base) - jnp.log1p(-a_base)
    lru_wa = nrm((n_l, 2, LRU_BLOCKS, bw, bw), bw ** -0.5)
    lru_ba = nrm((n_l, 2, LRU_W), 0.01)
    lru_wx = nrm((n_l, 2, LRU_BLOCKS, bw, bw), bw ** -0.5)
    lru_bx = nrm((n_l, 2, LRU_W), 0.01)
    conf_conv_w = nrm((n_l, CONF_KERNEL, CONF_W), CONF_KERNEL ** -0.5)
    conf_conv_b = nrm((n_l, CONF_W), 0.01)
    conf_ln_g = 1.0 + nrm((n_l, CONF_W), 0.02)
    conf_ln_b = nrm((n_l, CONF_W), 0.01)
    w_branch = nrm((n_l, N_BRANCH, BRANCH_W, D), BRANCH_W ** -0.5)
    gate_w = nrm((n_l, N_BRANCH, GATE_BLOCKS, gw, gw), gw ** -0.5)
    gate_b = nrm((n_l, N_BRANCH, D), 0.01)
    w_out = nrm((n_l, D, D), D ** -0.5)
    ffn_w_gate = nrm((n_d, D, FFN_DENSE), D ** -0.5)
    ffn_w_up = nrm((n_d, D, FFN_DENSE), D ** -0.5)
    ffn_w_down = nrm((n_d, FFN_DENSE, D), FFN_DENSE ** -0.5)
    router_w = nrm((n_m, D, N_EXPERTS), D ** -0.5)
    router_b = nrm((n_m, N_EXPERTS), 0.01)
    moe_w_gate = nrm((n_m, N_EXPERTS, D, FFN_EXPERT), D ** -0.5)
    moe_w_up = nrm((n_m, N_EXPERTS, D, FFN_EXPERT), D ** -0.5)
    moe_w_down = nrm((n_m, N_EXPERTS, FFN_EXPERT, D), FFN_EXPERT ** -0.5)
    return {'x': x, 'c': c, 'ctx': ctx, 'c_ctx': c_ctx, 'ada_w': ada_w, 'ada_b': ada_b,
            'norm_mix_g': norm_mix_g, 'norm_ffn_g': norm_ffn_g, 'w_in': w_in, 'qk_norm_g': qk_norm_g,
            'attn_sink': attn_sink, 'lru_conv_w': lru_conv_w, 'lru_conv_b': lru_conv_b,
            'lru_lambda': lru_lambda, 'lru_wa': lru_wa, 'lru_ba': lru_ba, 'lru_wx': lru_wx, 'lru_bx': lru_bx,
            'conf_conv_w': conf_conv_w, 'conf_conv_b': conf_conv_b, 'conf_ln_g': conf_ln_g,
            'conf_ln_b': conf_ln_b, 'w_branch': w_branch, 'gate_w': gate_w, 'gate_b': gate_b, 'w_out': w_out,
            'ffn_w_gate': ffn_w_gate, 'ffn_w_up': ffn_w_up, 'ffn_w_down': ffn_w_down,
            'router_w': router_w, 'router_b': router_b, 'moe_w_gate': moe_w_gate, 'moe_w_up': moe_w_up,
            'moe_w_down': moe_w_down}


def reference(x, c, ctx, c_ctx, ada_w, ada_b, norm_mix_g, norm_ffn_g, w_in, qk_norm_g, attn_sink,
              lru_conv_w, lru_conv_b, lru_lambda, lru_wa, lru_ba, lru_wx, lru_bx,
              conf_conv_w, conf_conv_b, conf_ln_g, conf_ln_b, w_branch, gate_w, gate_b, w_out,
              ffn_w_gate, ffn_w_up, ffn_w_down, router_w, router_b, moe_w_gate, moe_w_up, moe_w_down):
    L = x.shape[1]
    rope = axial_rope_tables(L)
    sc = jax.nn.silu(c)
    scc = jax.nn.silu(c_ctx)
    for l in range(DEPTH):
        need_ctx = l < DEPTH - 1
        mod = jnp.split((sc @ ada_w[l] + ada_b[l])[:, None, :], 6, axis=-1)
        mod_c = jnp.split(scc @ ada_w[l] + ada_b[l], 6, axis=-1)
        h_l = modulate(rms_norm(x, norm_mix_g[l]), mod[0], mod[1])
        h_c = modulate(rms_norm(ctx, norm_mix_g[l]), mod_c[0], mod_c[1])
        y_l, y_c = token_mixer(h_l, h_c, rope, need_ctx, w_in[l], qk_norm_g[l], attn_sink[l],
                               lru_conv_w[l], lru_conv_b[l], lru_lambda[l], lru_wa[l], lru_ba[l],
                               lru_wx[l], lru_bx[l], conf_conv_w[l], conf_conv_b[l], conf_ln_g[l],
                               conf_ln_b[l], w_branch[l], gate_w[l], gate_b[l], w_out[l])
        x = x + mod[2] * y_l
        h_l = modulate(rms_norm(x, norm_ffn_g[l]), mod[3], mod[4])
        x = x + mod[5] * channel_mixer(h_l, l, ffn_w_gate, ffn_w_up, ffn_w_down, router_w, router_b,
                                       moe_w_gate, moe_w_up, moe_w_down)
        if need_ctx:
            ctx = ctx + mod_c[2] * y_c
            h_c = modulate(rms_norm(ctx, norm_ffn_g[l]), mod_c[3], mod_c[4])
            ctx = ctx + mod_c[5] * channel_mixer(h_c, l, ffn_w_gate, ffn_w_up, ffn_w_down, router_w,
                                                 router_b, moe_w_gate, moe_w_up, moe_w_down)
    return x
```

```python
import functools
import math

import numpy as np
import jax
import jax.numpy as jnp
from jax import lax
from jax.experimental import pallas as pl
from jax.experimental.pallas import tpu as pltpu

F32 = jnp.float32
BF16 = jnp.bfloat16

HEAD_DIM = 128
N_Q_HEADS = 8
N_KV_HEADS = 2
GROUP = N_Q_HEADS // N_KV_HEADS
ATT_BLOCK = 128
GRID_W = 64
ROPE_THETA = 10000.0
NEG_INF = -1e30
LRU_C = 8.0
LRU_CONV = 4
CONF_KERNEL = 31
FOURIER_GROUP_W = 256
GATE_BLOCK_W = 256
N_EXPERTS = 8
EPS = 1e-6
SLAB = 512
SLAB_Q, SLAB_KV, SLAB_U, SLAB_G, SLAB_F, SLAB_CA, SLAB_CG = 0, 2, 3, 5, 7, 9, 11

V7X_VMEM_LIMIT = 56 * 1024 * 1024
LANES = 128


def _cp(*sem):
    return pltpu.CompilerParams(dimension_semantics=sem, vmem_limit_bytes=V7X_VMEM_LIMIT)


def _sigmoid(x):
    return 1.0 / (1.0 + jnp.exp(-x))


def _dot(a, b):
    return jnp.dot(a, b, preferred_element_type=F32)


def _dot_nt(a, b):
    return lax.dot_general(a, b, (((1,), (1,)), ((), ())), preferred_element_type=F32)


def _split_bf16(x):
    hi = x.astype(BF16)
    lo = (x - hi.astype(F32)).astype(BF16)
    return hi, lo


def _adaln_kernel(c_ref, w_ref, b_ref, o_ref):
    c = c_ref[...]
    s_hi, s_lo = _split_bf16(c * _sigmoid(c))
    w_hi, w_lo = _split_bf16(w_ref[...])
    o_ref[...] = _dot(s_hi, w_hi) + _dot(s_lo, w_hi) + _dot(s_hi, w_lo) + b_ref[...]


def _adaln(c_rows, ada_w, ada_b):
    depth, d, n = ada_w.shape
    tn = 512
    return pl.pallas_call(
        _adaln_kernel,
        grid=(depth, n // tn),
        in_specs=[pl.BlockSpec((8, d), lambda l, j: (0, 0)),
                  pl.BlockSpec((None, d, tn), lambda l, j: (l, 0, j)),
                  pl.BlockSpec((None, 1, tn), lambda l, j: (l, 0, j))],
        out_specs=pl.BlockSpec((None, 8, tn), lambda l, j: (l, 0, j)),
        out_shape=jax.ShapeDtypeStruct((depth, 8, n), F32),
        compiler_params=_cp("arbitrary", "arbitrary"),
        name="adaln",
    )(c_rows, ada_w, ada_b.reshape(depth, 1, n))


def _norm_mod(x_ref, g_ref, sh_ref, sc_ref, row):
    x = x_ref[...]
    y = x * lax.rsqrt(jnp.mean(x * x, axis=-1, keepdims=True) + EPS) * g_ref[...]
    return y * (1.0 + sc_ref[row:row + 1, :]) + sh_ref[row:row + 1, :]


def _norm_kernel(x_ref, g_ref, sh_ref, sc_ref, o_ref, *, row):
    o_ref[...] = _norm_mod(x_ref, g_ref, sh_ref, sc_ref, row).astype(BF16)


def _norm_router_kernel(x_ref, g_ref, sh_ref, sc_ref, rw_ref, rb_ref, o_ref, gates_ref, *, row):
    h = _norm_mod(x_ref, g_ref, sh_ref, sc_ref, row)
    o_ref[...] = h.astype(BF16)
    h_hi, h_lo = _split_bf16(h)
    w_hi, w_lo = _split_bf16(rw_ref[...])
    logits = _dot(h_hi, w_hi) + _dot(h_lo, w_hi) + _dot(h_hi, w_lo) + rb_ref[...]
    lane = lax.broadcasted_iota(jnp.int32, logits.shape, 1)
    logits = jnp.where(lane < N_EXPERTS, logits, -jnp.inf)
    m1 = jnp.max(logits, axis=-1, keepdims=True)
    i1 = jnp.min(jnp.where(logits == m1, lane, LANES), axis=-1, keepdims=True)
    rest = jnp.where(lane == i1, -jnp.inf, logits)
    m2 = jnp.max(rest, axis=-1, keepdims=True)
    i2 = jnp.min(jnp.where(rest == m2, lane, LANES), axis=-1, keepdims=True)
    e2 = jnp.exp(m2 - m1)
    inv = 1.0 / (1.0 + e2)
    gates_ref[...] = jnp.where(lane == i1, inv, 0.0) + jnp.where(lane == i2, e2 * inv, 0.0)


def _norm(x, g_all, mod, layer, chunk, row, router=None):
    m, d = x.shape
    tm = min(512, m)
    in_specs = [pl.BlockSpec((tm, d), lambda i: (i, 0)),
                pl.BlockSpec((None, 1, d), lambda i: (layer, 0, 0)),
                pl.BlockSpec((None, 8, d), lambda i: (layer, 0, chunk)),
                pl.BlockSpec((None, 8, d), lambda i: (layer, 0, chunk + 1))]
    args = [x, g_all.reshape(g_all.shape[0], 1, d), mod, mod]
    h_spec = pl.BlockSpec((tm, d), lambda i: (i, 0))
    h_shape = jax.ShapeDtypeStruct((m, d), BF16)
    if router is None:
        return pl.pallas_call(
            functools.partial(_norm_kernel, row=row), grid=(m // tm,), in_specs=in_specs,
            out_specs=h_spec, out_shape=h_shape, compiler_params=_cp("parallel"), name="norm",
        )(*args)
    rw, rb, li = router
    in_specs += [pl.BlockSpec((None, d, LANES), lambda i: (li, 0, 0)),
                 pl.BlockSpec((None, 1, LANES), lambda i: (li, 0, 0))]
    return pl.pallas_call(
        functools.partial(_norm_router_kernel, row=row), grid=(m // tm,), in_specs=in_specs,
        out_specs=[h_spec, pl.BlockSpec((tm, LANES), lambda i: (i, 0))],
        out_shape=[h_shape, jax.ShapeDtypeStruct((m, LANES), F32)],
        compiler_params=_cp("parallel"), name="norm_router",
    )(*args, rw, rb)


def _mm_kernel(a_ref, w_ref, o_ref):
    o_ref[...] = _dot(a_ref[...], w_ref[...]).astype(o_ref.dtype)


def _in_proj(h, w_all, layer):
    m, d = h.shape
    n = w_all.shape[-1]
    tm = min(1024, m)
    return pl.pallas_call(
        _mm_kernel,
        grid=(m // tm, n // SLAB),
        in_specs=[pl.BlockSpec((tm, d), lambda i, j: (i, 0)),
                  pl.BlockSpec((None, d, SLAB), lambda i, j: (layer, 0, j))],
        out_specs=pl.BlockSpec((None, tm, SLAB), lambda i, j: (j, i, 0)),
        out_shape=jax.ShapeDtypeStruct((n // SLAB, m, SLAB), BF16),
        compiler_params=_cp("parallel", "arbitrary"), name="in_proj",
    )(h, w_all)


def _mm_res_kernel(a_ref, w_ref, x_ref, gate_ref, o_ref, *, row):
    y = _dot(a_ref[...], w_ref[...])
    o_ref[...] = x_ref[...] + gate_ref[row:row + 1, :] * y


def _mm_res(a, w_all, layer, x, mod, mod_layer, chunk, row):
    m, k = a.shape
    d = x.shape[-1]
    tm = min(1024, m)
    tn = 512
    gate_off = chunk * d // tn
    return pl.pallas_call(
        functools.partial(_mm_res_kernel, row=row),
        grid=(m // tm, d // tn),
        in_specs=[pl.BlockSpec((tm, k), lambda i, j: (i, 0)),
                  pl.BlockSpec((None, k, tn), lambda i, j: (layer, 0, j)),
                  pl.BlockSpec((tm, tn), lambda i, j: (i, j)),
                  pl.BlockSpec((None, 8, tn), lambda i, j: (mod_layer, 0, gate_off + j))],
        out_specs=pl.BlockSpec((tm, tn), lambda i, j: (i, j)),
        out_shape=jax.ShapeDtypeStruct((m, d), F32),
        compiler_params=_cp("parallel", "arbitrary"), name="mm_res",
    )(a, w_all, x, mod)


def _ffn_up_kernel(h_ref, wg_ref, wu_ref, o_ref):
    h = h_ref[...]
    g = _dot(h, wg_ref[...])
    o_ref[...] = ((g * _sigmoid(g)) * _dot(h, wu_ref[...])).astype(o_ref.dtype)


def _moe_up_kernel(h_ref, wg_ref, wu_ref, gates_ref, o_ref):
    h = h_ref[...]
    g = _dot(h, wg_ref[...])
    hid = (g * _sigmoid(g)) * _dot(h, wu_ref[...])
    gates = gates_ref[...]
    lane = lax.broadcasted_iota(jnp.int32, gates.shape, 1)
    gate = jnp.sum(jnp.where(lane == pl.program_id(1), gates, 0.0), axis=-1, keepdims=True)
    o_ref[...] = (hid * gate).astype(o_ref.dtype)


def _ffn_up(h, wg_all, wu_all, li):
    m, d = h.shape
    f = wg_all.shape[-1]
    tm = min(1024, m)
    tn = 512
    w_spec = pl.BlockSpec((None, d, tn), lambda i, j: (li, 0, j))
    return pl.pallas_call(
        _ffn_up_kernel, grid=(m // tm, f // tn),
        in_specs=[pl.BlockSpec((tm, d), lambda i, j: (i, 0)), w_spec, w_spec],
        out_specs=pl.BlockSpec((tm, tn), lambda i, j: (i, j)),
        out_shape=jax.ShapeDtypeStruct((m, f), BF16),
        compiler_params=_cp("parallel", "arbitrary"), name="ffn_up",
    )(h, wg_all, wu_all)


def _moe_up(h, gates, wg_all, wu_all, li):
    m, d = h.shape
    n_e, fe = wg_all.shape[1], wg_all.shape[-1]
    tm = min(1024, m)
    w_spec = pl.BlockSpec((None, None, d, fe), lambda i, e: (li, e, 0, 0))
    return pl.pallas_call(
        _moe_up_kernel, grid=(m // tm, n_e),
        in_specs=[pl.BlockSpec((tm, d), lambda i, e: (i, 0)), w_spec, w_spec,
                  pl.BlockSpec((tm, LANES), lambda i, e: (i, 0))],
        out_specs=pl.BlockSpec((tm, fe), lambda i, e: (i, e)),
        out_shape=jax.ShapeDtypeStruct((m, n_e * fe), BF16),
        compiler_params=_cp("parallel", "arbitrary"), name="moe_up",
    )(h, wg_all, wu_all, gates)


def _rope_tables(length):
    rows = length // GRID_W
    row = jnp.broadcast_to(jnp.arange(rows, dtype=F32)[:, None], (rows, GRID_W)).reshape(-1)
    col = jnp.broadcast_to(jnp.arange(GRID_W, dtype=F32)[None, :], (rows, GRID_W)).reshape(-1)
    n_freq = HEAD_DIM // 4
    inv_freq = ROPE_THETA ** (-jnp.arange(n_freq, dtype=F32) / n_freq)
    ang_r = row[:, None] * inv_freq
    ang_c = col[:, None] * inv_freq
    cr, sr, cc, sc = jnp.cos(ang_r), jnp.sin(ang_r), jnp.cos(ang_c), jnp.sin(ang_c)
    return (jnp.concatenate([cr, cr, cc, cc], axis=-1), jnp.concatenate([-sr, sr, -sc, sc], axis=-1))


def _qk_prep_kernel(q0_ref, q1_ref, kv_ref, cos_ref, sin_ref, g_ref, qn_ref, kn_ref):
    cos = cos_ref[...]
    sin = sin_ref[...]
    t = cos.shape[0]
    lane = lax.broadcasted_iota(jnp.int32, (t, HEAD_DIM), 1)
    low_quarter = (lane & 63) < 32

    def prep(xh, g):
        xh = xh.astype(F32)
        y = xh * lax.rsqrt(jnp.mean(xh * xh, axis=-1, keepdims=True) + EPS) * g
        partner = jnp.where(low_quarter, pltpu.roll(y, HEAD_DIM - 32, 1), pltpu.roll(y, 32, 1))
        return (y * cos + partner * sin).astype(BF16)

    qg = g_ref[0:1, :]
    kg = g_ref[1:2, :]
    for h in range(N_Q_HEADS):
        src = q0_ref if h < GROUP else q1_ref
        c0 = (h % GROUP) * HEAD_DIM
        qh = prep(src[:, c0:c0 + HEAD_DIM], qg)
        for b in range(t // ATT_BLOCK):
            qn_ref[b, h] = qh[b * ATT_BLOCK:(b + 1) * ATT_BLOCK]
    for h in range(N_KV_HEADS):
        c0 = h * HEAD_DIM
        kn_ref[:, c0:c0 + HEAD_DIM] = prep(kv_ref[:, c0:c0 + HEAD_DIM], kg)


def _qk_prep(proj, cos, sin, g_all, layer):
    m = proj.shape[1]
    t = min(512, m)
    tb = t // ATT_BLOCK
    slab = lambda s: pl.BlockSpec((None, t, SLAB), lambda i: (s, i, 0))
    tab = pl.BlockSpec((t, HEAD_DIM), lambda i: (i, 0))
    return pl.pallas_call(
        _qk_prep_kernel, grid=(m // t,),
        in_specs=[slab(SLAB_Q), slab(SLAB_Q + 1),
                  pl.BlockSpec((None, t, N_KV_HEADS * HEAD_DIM), lambda i: (SLAB_KV, i, 0)),
                  tab, tab, pl.BlockSpec((None, 2, HEAD_DIM), lambda i: (layer, 0, 0))],
        out_specs=[pl.BlockSpec((tb, N_Q_HEADS, ATT_BLOCK, HEAD_DIM), lambda i: (i, 0, 0, 0)),
                   pl.BlockSpec((t, N_KV_HEADS * HEAD_DIM), lambda i: (i, 0))],
        out_shape=[jax.ShapeDtypeStruct((m // ATT_BLOCK, N_Q_HEADS, ATT_BLOCK, HEAD_DIM), BF16),
                   jax.ShapeDtypeStruct((m, N_KV_HEADS * HEAD_DIM), BF16)],
        compiler_params=_cp("parallel"), name="qk_prep",
    )(proj, proj, proj, cos, sin, g_all)


def _sink_col(sink_ref, kvh):
    blk = lax.broadcasted_iota(jnp.int32, (GROUP * ATT_BLOCK, 1), 0) >> (ATT_BLOCK.bit_length() - 1)
    col = jnp.full(blk.shape, sink_ref[kvh * GROUP + GROUP - 1], F32)
    for g in range(GROUP - 2, -1, -1):
        col = jnp.where(blk == g, sink_ref[kvh * GROUP + g], col)
    return col


def _softmax_pv(scores, values, sink_col):
    m = sink_col
    for s in scores:
        m = jnp.maximum(m, jnp.max(s, axis=-1, keepdims=True))
    den = jnp.exp(sink_col - m)
    out = None
    for s, v in zip(scores, values):
        e = jnp.exp(s - m)
        den = den + jnp.sum(e, axis=-1, keepdims=True)
        pv = _dot(e.astype(BF16), v)
        out = pv if out is None else out + pv
    return out * (1.0 / den)


def _attn_kernel(sink_ref, q_ref, kp_ref, kc_ref, kn_ref, vp_ref, vc_ref, vn_ref, kx_ref, vx_ref,
                 o_ref, kwin, vwin, *, n_blocks):
    i = pl.program_id(0)
    tb = q_ref.shape[0]
    tq = tb * ATT_BLOCK
    for win, p, c, n in ((kwin, kp_ref, kc_ref, kn_ref), (vwin, vp_ref, vc_ref, vn_ref)):
        win[0:ATT_BLOCK] = p[...]
        win[ATT_BLOCK:ATT_BLOCK + tq] = c[...]
        win[ATT_BLOCK + tq:] = n[...]
    scale = HEAD_DIM ** -0.5
    shape = (GROUP * ATT_BLOCK, 3 * ATT_BLOCK)
    qpos = lax.broadcasted_iota(jnp.int32, shape, 0) & (ATT_BLOCK - 1)
    kpos = lax.broadcasted_iota(jnp.int32, shape, 1)
    band = (kpos >= qpos) & (kpos <= qpos + 2 * ATT_BLOCK)
    for b in range(tb):
        blk = i * tb + b
        lo = jnp.where(blk == 0, ATT_BLOCK, 0)
        hi = jnp.where(blk == n_blocks - 1, 2 * ATT_BLOCK, 3 * ATT_BLOCK)
        valid = band & (kpos >= lo) & (kpos < hi)
        r0 = b * ATT_BLOCK
        for kvh in range(N_KV_HEADS):
            c0 = kvh * HEAD_DIM
            q4 = q_ref[b, kvh * GROUP:(kvh + 1) * GROUP].reshape(GROUP * ATT_BLOCK, HEAD_DIM)
            s_loc = _dot_nt(q4, kwin[r0:r0 + 3 * ATT_BLOCK, c0:c0 + HEAD_DIM]) * scale
            s_loc = jnp.where(valid, s_loc, NEG_INF)
            s_ctx = _dot_nt(q4, kx_ref[:, c0:c0 + HEAD_DIM]) * scale
            o = _softmax_pv((s_loc, s_ctx),
                            (vwin[r0:r0 + 3 * ATT_BLOCK, c0:c0 + HEAD_DIM], vx_ref[:, c0:c0 + HEAD_DIM]),
                            _sink_col(sink_ref, kvh))
            for g in range(GROUP):
                h0 = (kvh * GROUP + g) * HEAD_DIM
                o_ref[r0:r0 + ATT_BLOCK, h0:h0 + HEAD_DIM] = (
                    o[g * ATT_BLOCK:(g + 1) * ATT_BLOCK].astype(o_ref.dtype))


def _attention(qn, kn, proj, kn_ctx, proj_ctx, sink):
    nb = qn.shape[0]
    m = nb * ATT_BLOCK
    cx = kn_ctx.shape[0]
    tb = min(4, nb)
    tq = tb * ATT_BLOCK
    kvw = N_KV_HEADS * HEAD_DIM
    prev = lambda i: jnp.maximum(i * tb - 1, 0)
    nxt = lambda i: jnp.minimum((i + 1) * tb, nb - 1)
    return pl.pallas_call(
        functools.partial(_attn_kernel, n_blocks=nb), grid=(nb // tb,),
        in_specs=[pl.BlockSpec(memory_space=pltpu.SMEM),
                  pl.BlockSpec((tb, N_Q_HEADS, ATT_BLOCK, HEAD_DIM), lambda i: (i, 0, 0, 0)),
                  pl.BlockSpec((ATT_BLOCK, kvw), lambda i: (prev(i), 0)),
                  pl.BlockSpec((tq, kvw), lambda i: (i, 0)),
                  pl.BlockSpec((ATT_BLOCK, kvw), lambda i: (nxt(i), 0)),
                  pl.BlockSpec((None, ATT_BLOCK, kvw), lambda i: (SLAB_KV, prev(i), 1)),
                  pl.BlockSpec((None, tq, kvw), lambda i: (SLAB_KV, i, 1)),
                  pl.BlockSpec((None, ATT_BLOCK, kvw), lambda i: (SLAB_KV, nxt(i), 1)),
                  pl.BlockSpec((cx, kvw), lambda i: (0, 0)),
                  pl.BlockSpec((None, cx, kvw), lambda i: (SLAB_KV, 0, 1))],
        out_specs=pl.BlockSpec((tq, N_Q_HEADS * HEAD_DIM), lambda i: (i, 0)),
        out_shape=jax.ShapeDtypeStruct((m, N_Q_HEADS * HEAD_DIM), BF16),
        scratch_shapes=[pltpu.VMEM((tq + 2 * ATT_BLOCK, kvw), BF16),
                        pltpu.VMEM((tq + 2 * ATT_BLOCK, kvw), BF16)],
        compiler_params=_cp("parallel"), name="attention",
    )(sink, qn, kn, kn, kn, proj, proj, proj, kn_ctx, proj_ctx)


def _ctx_attn_kernel(sink_ref, q_ref, k_ref, v_ref, o_ref):
    scale = HEAD_DIM ** -0.5
    for b in range(q_ref.shape[0]):
        r0 = b * ATT_BLOCK
        for kvh in range(N_KV_HEADS):
            c0 = kvh * HEAD_DIM
            q4 = q_ref[b, kvh * GROUP:(kvh + 1) * GROUP].reshape(GROUP * ATT_BLOCK, HEAD_DIM)
            s = _dot_nt(q4, k_ref[:, c0:c0 + HEAD_DIM]) * scale
            o = _softmax_pv((s,), (v_ref[:, c0:c0 + HEAD_DIM],), _sink_col(sink_ref, kvh))
            for g in range(GROUP):
                h0 = (kvh * GROUP + g) * HEAD_DIM
                o_ref[r0:r0 + ATT_BLOCK, h0:h0 + HEAD_DIM] = (
                    o[g * ATT_BLOCK:(g + 1) * ATT_BLOCK].astype(o_ref.dtype))


def _ctx_attention(qn, kn, proj, sink):
    nb = qn.shape[0]
    cx = nb * ATT_BLOCK
    kvw = N_KV_HEADS * HEAD_DIM
    return pl.pallas_call(
        _ctx_attn_kernel, grid=(1,),
        in_specs=[pl.BlockSpec(memory_space=pltpu.SMEM),
                  pl.BlockSpec((nb, N_Q_HEADS, ATT_BLOCK, HEAD_DIM), lambda i: (0, 0, 0, 0)),
                  pl.BlockSpec((cx, kvw), lambda i: (0, 0)),
                  pl.BlockSpec((None, cx, kvw), lambda i: (SLAB_KV, 0, 1))],
        out_specs=pl.BlockSpec((cx, N_Q_HEADS * HEAD_DIM), lambda i: (0, 0)),
        out_shape=jax.ShapeDtypeStruct((cx, N_Q_HEADS * HEAD_DIM), BF16),
        compiler_params=_cp("arbitrary"), name="ctx_attention",
    )(sink, qn, kn, proj)


def _softplus(z):
    return jnp.maximum(z, 0.0) + jnp.log1p(jnp.exp(-jnp.abs(z)))


def _gelu_tanh(x):
    return 0.5 * x * (1.0 + jnp.tanh(math.sqrt(2.0 / math.pi) * (x + 0.044715 * (x * x * x))))


def _lru_kernel(*refs, reverse, n_t, fuse):
    (u_ref, up_ref, un_ref, cw_ref, cb_ref, lam_ref, wa_ref, ba_ref, wx_ref, bx_ref, h0_ref) = refs[:11]
    if fuse:
        hf_ref, g_ref, o_ref, hfin_ref, ext, a_s, b_s, o_s, carry = refs[11:]
    else:
        o_ref, hfin_ref, ext, a_s, b_s, carry = refs[11:]
        o_s = o_ref
    i = pl.program_id(1)
    tblk = (n_t - 1 - i) if reverse else i
    t = u_ref.shape[0]
    halo = 8
    up = up_ref[...].astype(F32)[up_ref.shape[0] - halo:]
    un = un_ref[...].astype(F32)[:halo]
    ext[0:halo] = jnp.where(tblk == 0, 0.0, up)
    ext[halo:halo + t] = u_ref[...].astype(F32)
    ext[halo + t:] = jnp.where(tblk == n_t - 1, 0.0, un)
    x = cb_ref[...] + cw_ref[0:1, :] * ext[halo - 1:halo - 1 + t]
    for k in range(1, LRU_CONV):
        x = x + cw_ref[k:k + 1, :] * ext[halo - 1 + k:halo - 1 + k + t]
    xb = x.astype(BF16)
    w = HEAD_DIM
    r_pre = jnp.concatenate([_dot(xb[:, n * w:(n + 1) * w], wa_ref[n]) for n in range(wa_ref.shape[0])], axis=-1)
    i_pre = jnp.concatenate([_dot(xb[:, n * w:(n + 1) * w], wx_ref[n]) for n in range(wx_ref.shape[0])], axis=-1)
    r = _sigmoid(r_pre + ba_ref[...])
    ig = _sigmoid(i_pre + bx_ref[...])
    log_a = (-LRU_C * _softplus(-lam_ref[...])) * r
    a = jnp.exp(log_a)
    a_s[...] = a
    b_s[...] = jnp.sqrt(-jnp.tanh(log_a) * (a * a + 1.0)) * (ig * x)

    @pl.when(i == 0)
    def _():
        carry[...] = h0_ref[...]

    def step(s, h):
        row = (t - 1 - s) if reverse else s
        h = a_s[pl.ds(row, 1), :] * h + b_s[pl.ds(row, 1), :]
        o_s[pl.ds(row, 1), :] = h
        return h

    h = lax.fori_loop(0, t, step, carry[0:1, :], unroll=8)
    carry[...] = jnp.broadcast_to(h, carry.shape)
    hfin_ref[...] = jnp.broadcast_to(h, hfin_ref.shape)
    if fuse:
        o_ref[...] = ((hf_ref[...] + o_s[...]) * _gelu_tanh(g_ref[...].astype(F32))).astype(o_ref.dtype)


def _lru(proj, layer, direction, h0, conv_w, conv_b, lam, wa, ba, wx, bx, h_first=None):
    m = proj.shape[1]
    width = conv_w.shape[-1]
    n_slab = width // SLAB
    t = min(1024, m)
    n_t = m // t
    hb = 16
    reverse = direction == 1
    fuse = h_first is not None
    tmap = (lambda i: n_t - 1 - i) if reverse else (lambda i: i)
    n_gate_blk = SLAB // HEAD_DIM
    vec = lambda a: a.reshape(a.shape[:-1] + (1, width))
    dvec = pl.BlockSpec((None, None, 1, SLAB), lambda s, i: (layer, direction, 0, s))
    dmat = pl.BlockSpec((None, None, n_gate_blk, HEAD_DIM, HEAD_DIM), lambda s, i: (layer, direction, s, 0, 0))
    in_specs = [
        pl.BlockSpec((None, t, SLAB), lambda s, i: (SLAB_U + s, tmap(i), 0)),
        pl.BlockSpec((None, hb, SLAB), lambda s, i: (SLAB_U + s, jnp.maximum(tmap(i) * (t // hb) - 1, 0), 0)),
        pl.BlockSpec((None, hb, SLAB), lambda s, i: (SLAB_U + s, jnp.minimum((tmap(i) + 1) * (t // hb), m // hb - 1), 0)),
        pl.BlockSpec((None, LRU_CONV, SLAB), lambda s, i: (layer, 0, s)),
        pl.BlockSpec((None, 1, SLAB), lambda s, i: (layer, 0, s)),
        dvec, dmat, dvec, dmat, dvec,
        pl.BlockSpec((8, SLAB), lambda s, i: (0, s)),
    ]
    args = [proj, proj, proj, conv_w, vec(conv_b), vec(lam), wa, vec(ba), wx, vec(bx), h0]
    seq_spec = pl.BlockSpec((t, SLAB), lambda s, i: (tmap(i), s))
    scratch = [pltpu.VMEM((t + 16, SLAB), F32), pltpu.VMEM((t, SLAB), F32), pltpu.VMEM((t, SLAB), F32)]
    if fuse:
        in_specs += [seq_spec, pl.BlockSpec((None, t, SLAB), lambda s, i: (SLAB_G + s, tmap(i), 0))]
        args += [h_first, proj]
        scratch.append(pltpu.VMEM((t, SLAB), F32))
    scratch.append(pltpu.VMEM((8, SLAB), F32))
    return pl.pallas_call(
        functools.partial(_lru_kernel, reverse=reverse, n_t=n_t, fuse=fuse),
        grid=(n_slab, n_t), in_specs=in_specs,
        out_specs=[seq_spec, pl.BlockSpec((8, SLAB), lambda s, i: (0, s))],
        out_shape=[jax.ShapeDtypeStruct((m, width), BF16 if fuse else F32),
                   jax.ShapeDtypeStruct((8, width), F32)],
        scratch_shapes=scratch,
        compiler_params=_cp("arbitrary", "arbitrary"), name="lru",
    )(*args)


def _dft_tables(n):
    ang = 2.0 * np.pi * np.outer(np.arange(n), np.arange(n)) / n
    return np.cos(ang), np.sin(ang)


def _fourier1_kernel(x_ref, f_ref, twc_ref, tws_ref, br_ref, bi_ref):
    n1 = x_ref.shape[0]
    a = _dot(f_ref[...], x_ref[...])
    ar, ai = a[:n1], a[n1:]
    for j in range(twc_ref.shape[0]):
        cs = slice(j * SLAB, (j + 1) * SLAB)
        tc, ts = twc_ref[j], tws_ref[j]
        br_ref[:, cs] = (ar[:, cs] * tc + ai[:, cs] * ts).astype(br_ref.dtype)
        bi_ref[:, cs] = (ai[:, cs] * tc - ar[:, cs] * ts).astype(bi_ref.dtype)


def _fourier2_kernel(br_ref, bi_ref, c_ref, s_ref, cc_ref, sc_ref, o_ref, *, scale):
    n_slab, kb = br_ref.shape[0], br_ref.shape[1]
    c, s = c_ref[...], s_ref[...]
    gw = FOURIER_GROUP_W
    for j in range(kb):
        for sl in range(n_slab):
            br, bi = br_ref[sl, j], bi_ref[sl, j]
            pr = (_dot(c, br) + _dot(s, bi)).astype(BF16)
            pi = (_dot(c, bi) - _dot(s, br)).astype(BF16)
            for g in range(SLAB // gw):
                y = _dot(pr[:, g * gw:(g + 1) * gw], cc_ref[...]) + _dot(pi[:, g * gw:(g + 1) * gw], sc_ref[...])
                c0 = (j * n_slab + sl) * SLAB + g * gw
                o_ref[:, c0:c0 + gw] = (y * scale).astype(o_ref.dtype)


def _fourier(proj):
    n_seg, m, _ = proj.shape
    n1 = 1 << ((m.bit_length() - 1) // 2)
    n2 = m // n1
    assert n1 * n2 == m and n1 % 16 == 0 and n2 % 16 == 0
    n_slab = 2
    width = n_slab * SLAB
    c1, s1 = _dft_tables(n1)
    f1 = jnp.asarray(np.concatenate([c1, -s1], axis=0), BF16)
    tau = 2.0 * np.pi * np.outer(np.arange(n2), np.arange(n1)) / m
    twc = jnp.asarray(np.cos(tau)[:, :, None], F32)
    tws = jnp.asarray(np.sin(tau)[:, :, None], F32)
    nl2 = min(8, n2)
    b_spec = pl.BlockSpec((None, n1, nl2 * SLAB), lambda s, j: (s, 0, j))
    b_shape = jax.ShapeDtypeStruct((n_slab, n1, n2 * SLAB), BF16)
    tw_spec = pl.BlockSpec((nl2, n1, 1), lambda s, j: (j, 0, 0))
    br, bi = pl.pallas_call(
        _fourier1_kernel, grid=(n_slab, n2 // nl2),
        in_specs=[pl.BlockSpec((None, n1, nl2 * SLAB), lambda s, j: (SLAB_F + s, 0, j)),
                  pl.BlockSpec((2 * n1, n1), lambda s, j: (0, 0)), tw_spec, tw_spec],
        out_specs=[b_spec, b_spec], out_shape=[b_shape, b_shape],
        compiler_params=_cp("parallel", "arbitrary"), name="fourier1",
    )(proj.reshape(n_seg, n1, n2 * SLAB), f1, twc, tws)

    c2, s2 = _dft_tables(n2)
    cc, sc = _dft_tables(FOURIER_GROUP_W)
    kb = min(8, n1)
    in4 = pl.BlockSpec((n_slab, kb, n2, SLAB), lambda k: (0, k, 0, 0))
    full = lambda n: pl.BlockSpec((n, n), lambda k: (0, 0))
    out = pl.pallas_call(
        functools.partial(_fourier2_kernel, scale=1.0 / math.sqrt(m * FOURIER_GROUP_W)),
        grid=(n1 // kb,),
        in_specs=[in4, in4, full(n2), full(n2), full(FOURIER_GROUP_W), full(FOURIER_GROUP_W)],
        out_specs=pl.BlockSpec((n2, kb * width), lambda k: (0, k)),
        out_shape=jax.ShapeDtypeStruct((n2, n1 * width), BF16),
        compiler_params=_cp("parallel"), name="fourier2",
    )(br.reshape(n_slab, n1, n2, SLAB), bi.reshape(n_slab, n1, n2, SLAB),
      jnp.asarray(c2, BF16), jnp.asarray(s2, BF16), jnp.asarray(cc, BF16), jnp.asarray(sc, BF16))
    return out.reshape(m, width)


def _fourier_direct_kernel(u0_ref, u1_ref, cl_ref, sl_ref, cc_ref, sc_ref, o_ref, *, scale):
    gw = FOURIER_GROUP_W
    for sl, u_ref in enumerate((u0_ref, u1_ref)):
        u = u_ref[...]
        pr = _dot(cl_ref[...], u).astype(BF16)
        pi = (-_dot(sl_ref[...], u)).astype(BF16)
        for g in range(SLAB // gw):
            y = _dot(pr[:, g * gw:(g + 1) * gw], cc_ref[...]) + _dot(pi[:, g * gw:(g + 1) * gw], sc_ref[...])
            c0 = sl * SLAB + g * gw
            o_ref[:, c0:c0 + gw] = (y * scale).astype(o_ref.dtype)


def _fourier_direct(proj):
    m = proj.shape[1]
    cl, sl = _dft_tables(m)
    cc, sc = _dft_tables(FOURIER_GROUP_W)
    full = lambda n: pl.BlockSpec((n, n), lambda i: (0, 0))
    return pl.pallas_call(
        functools.partial(_fourier_direct_kernel, scale=1.0 / math.sqrt(m * FOURIER_GROUP_W)),
        grid=(1,),
        in_specs=[pl.BlockSpec((None, m, SLAB), lambda i: (SLAB_F, 0, 0)),
                  pl.BlockSpec((None, m, SLAB), lambda i: (SLAB_F + 1, 0, 0)),
                  full(m), full(m), full(FOURIER_GROUP_W), full(FOURIER_GROUP_W)],
        out_specs=pl.BlockSpec((m, 2 * SLAB), lambda i: (0, 0)),
        out_shape=jax.ShapeDtypeStruct((m, 2 * SLAB), BF16),
        compiler_params=_cp("arbitrary"), name="fourier_direct",
    )(proj, proj, jnp.asarray(cl, BF16), jnp.asarray(sl, BF16), jnp.asarray(cc, BF16), jnp.asarray(sc, BF16))


CONF_HALO = 16
CONF_ROWS = 32


def _conformer_kernel(*refs, n_t):
    cur = refs[0:4]
    prv = refs[4:8]
    nxt = refs[8:12]
    cw_ref, cb_ref, lg_ref, lb_ref, o_ref, ext, shifted = refs[12:]
    i = pl.program_id(0)
    t = cur[0].shape[0]
    width = 2 * SLAB

    def glu(parts, keep):
        for sl in range(2):
            a = parts[sl][...].astype(F32)
            g = parts[2 + sl][...].astype(F32)
            yield sl, jnp.where(keep, a * _sigmoid(g), 0.0)

    for sl, v in glu(prv, i > 0):
        ext[0:CONF_HALO, sl * SLAB:(sl + 1) * SLAB] = v
    for sl, v in glu(cur, True):
        ext[CONF_HALO:CONF_HALO + t, sl * SLAB:(sl + 1) * SLAB] = v
    for sl, v in glu(nxt, i < n_t - 1):
        ext[CONF_HALO + t:, sl * SLAB:(sl + 1) * SLAB] = v
    rows = shifted.shape[1]
    for j in range(8):
        shifted[j] = ext[j:j + rows]

    n_ct = width // LANES

    def chunk(rc, _):
        r0 = pl.multiple_of(rc * CONF_ROWS, CONF_ROWS)
        accs = []
        for c in range(n_ct):
            cs = slice(c * LANES, (c + 1) * LANES)
            acc = jnp.broadcast_to(cb_ref[:, cs], (CONF_ROWS, LANES))
            for k in range(CONF_KERNEL):
                off = k + 1
                acc = acc + cw_ref[k:k + 1, cs] * shifted[off % 8, pl.ds(r0 + (off - off % 8), CONF_ROWS), cs]
            accs.append(acc)
        tot = accs[0]
        for acc in accs[1:]:
            tot = tot + acc
        mu = jnp.sum(tot, axis=-1, keepdims=True) * (1.0 / width)
        ds = [acc - mu for acc in accs]
        sq = ds[0] * ds[0]
        for d in ds[1:]:
            sq = sq + d * d
        inv = lax.rsqrt(jnp.sum(sq, axis=-1, keepdims=True) * (1.0 / width) + EPS)
        for c in range(n_ct):
            cs = slice(c * LANES, (c + 1) * LANES)
            y = ds[c] * inv * lg_ref[:, cs] + lb_ref[:, cs]
            o_ref[pl.ds(r0, CONF_ROWS), cs] = (y * _sigmoid(y)).astype(o_ref.dtype)
        return 0

    lax.fori_loop(0, t // CONF_ROWS, chunk, 0)


def _conformer(proj, layer, conv_w, conv_b, ln_g, ln_b):
    m = proj.shape[1]
    width = conv_w.shape[-1]
    t = min(256, m)
    n_t = m // t
    hb = CONF_HALO
    cur = lambda s: pl.BlockSpec((None, t, SLAB), lambda i: (s, i, 0))
    prv = lambda s: pl.BlockSpec((None, hb, SLAB), lambda i: (s, jnp.maximum(i * (t // hb) - 1, 0), 0))
    nxt = lambda s: pl.BlockSpec((None, hb, SLAB), lambda i: (s, jnp.minimum((i + 1) * (t // hb), m // hb - 1), 0))
    slabs = (SLAB_CA, SLAB_CA + 1, SLAB_CG, SLAB_CG + 1)
    vec = lambda a: a.reshape(a.shape[0], 1, width)
    vspec = pl.BlockSpec((None, 1, width), lambda i: (layer, 0, 0))
    return pl.pallas_call(
        functools.partial(_conformer_kernel, n_t=n_t), grid=(n_t,),
        in_specs=[cur(s) for s in slabs] + [prv(s) for s in slabs] + [nxt(s) for s in slabs]
        + [pl.BlockSpec((None, CONF_KERNEL, width), lambda i: (layer, 0, 0)), vspec, vspec, vspec],
        out_specs=pl.BlockSpec((t, width), lambda i: (i, 0)),
        out_shape=jax.ShapeDtypeStruct((m, width), BF16),
        scratch_shapes=[pltpu.VMEM((t + 2 * hb, width), F32), pltpu.VMEM((8, t + 2 * hb - 8, width), F32)],
        compiler_params=_cp("parallel"), name="conformer",
    )(*([proj] * 12), conv_w, vec(conv_b), vec(ln_g), vec(ln_b))


def _merge_kernel(h_ref, gw_ref, gb_ref, ya_ref, yb_ref, yc_ref, yd_ref, wb_ref, o_ref):
    h = h_ref[...]
    acc = None
    for b, y_ref in enumerate((ya_ref, yb_ref, yc_ref, yd_ref)):
        gate = _sigmoid(_dot(h, gw_ref[b]) + gb_ref[b:b + 1, :])
        term = gate * _dot(y_ref[...], wb_ref[b])
        acc = term if acc is None else acc + term
    o_ref[...] = acc.astype(o_ref.dtype)


def _merge(h, ys, gate_w, gate_b, w_branch, layer):
    m, d = h.shape
    bw = ys[0].shape[-1]
    nbr = len(ys)
    tm = min(1024, m)
    tn = GATE_BLOCK_W
    y_spec = pl.BlockSpec((tm, bw), lambda i, n: (i, 0))
    return pl.pallas_call(
        _merge_kernel, grid=(m // tm, d // tn),
        in_specs=[pl.BlockSpec((tm, tn), lambda i, n: (i, n)),
                  pl.BlockSpec((None, nbr, None, tn, tn), lambda i, n: (layer, 0, n, 0, 0)),
                  pl.BlockSpec((None, nbr, tn), lambda i, n: (layer, 0, n)),
                  y_spec, y_spec, y_spec, y_spec,
                  pl.BlockSpec((None, nbr, bw, tn), lambda i, n: (layer, 0, 0, n))],
        out_specs=pl.BlockSpec((tm, tn), lambda i, n: (i, n)),
        out_shape=jax.ShapeDtypeStruct((m, d), BF16),
        compiler_params=_cp("parallel", "arbitrary"), name="merge",
    )(h, gate_w, gate_b, *ys, w_branch)


def kernel(x, c, ctx, c_ctx, ada_w, ada_b, norm_mix_g, norm_ffn_g, w_in, qk_norm_g, attn_sink,
           lru_conv_w, lru_conv_b, lru_lambda, lru_wa, lru_ba, lru_wx, lru_bx,
           conf_conv_w, conf_conv_b, conf_ln_g, conf_ln_b, w_branch, gate_w, gate_b, w_out,
           ffn_w_gate, ffn_w_up, ffn_w_down, router_w, router_b, moe_w_gate, moe_w_up, moe_w_down):
    assert x.shape[0] == 1 and ctx.shape[0] == 1 and c.shape[0] == 1
    depth = ada_w.shape[0]
    seq, d = x.shape[1], x.shape[2]
    cx = ctx.shape[1]
    xl = x[0]
    xc = ctx[0]

    w_in_b, w_branch_b, gate_w_b, w_out_b = (a.astype(BF16) for a in (w_in, w_branch, gate_w, w_out))
    ffn_g_b, ffn_u_b, ffn_d_b = (a.astype(BF16) for a in (ffn_w_gate, ffn_w_up, ffn_w_down))
    moe_g_b, moe_u_b = moe_w_gate.astype(BF16), moe_w_up.astype(BF16)
    moe_d_b = moe_w_down.astype(BF16).reshape(moe_w_down.shape[0], -1, d)
    lru_wa_b, lru_wx_b = lru_wa.astype(BF16), lru_wx.astype(BF16)
    router_w_p = jnp.pad(router_w, ((0, 0), (0, 0), (0, LANES - N_EXPERTS)))
    router_b_p = jnp.pad(router_b, ((0, 0), (0, LANES - N_EXPERTS)))[:, None, :]

    c_rows = jnp.concatenate([c, c_ctx[None, :], jnp.zeros((6, d), F32)], axis=0)
    mod = _adaln(c_rows, ada_w, ada_b)

    cos_l, sin_l = _rope_tables(seq)
    cos_c, sin_c = jnp.ones((cx, HEAD_DIM), F32), jnp.zeros((cx, HEAD_DIM), F32)
    zero_state = jnp.zeros((8, lru_conv_w.shape[-1]), F32)
    lru_p = (lru_conv_w, lru_conv_b, lru_lambda, lru_wa_b, lru_ba, lru_wx_b, lru_bx)

    def channel_mixer(xs, l, row):
        li = l // 2
        if l % 2 == 0:
            h = _norm(xs, norm_ffn_g, mod, l, 3, row)
            hid = _ffn_up(h, ffn_g_b, ffn_u_b, li)
            return _mm_res(hid, ffn_d_b, li, xs, mod, l, 5, row)
        h, gates = _norm(xs, norm_ffn_g, mod, l, 3, row, router=(router_w_p, router_b_p, li))
        hid = _moe_up(h, gates, moe_g_b, moe_u_b, li)
        return _mm_res(hid, moe_d_b, li, xs, mod, l, 5, row)

    for l in range(depth):
        need_ctx = l < depth - 1
        h_l = _norm(xl, norm_mix_g, mod, l, 0, 0)
        h_c = _norm(xc, norm_mix_g, mod, l, 0, 1)
        p_l = _in_proj(h_l, w_in_b, l)
        p_c = _in_proj(h_c, w_in_b, l)
        qn_l, kn_l = _qk_prep(p_l, cos_l, sin_l, qk_norm_g, l)
        qn_c, kn_c = _qk_prep(p_c, cos_c, sin_c, qk_norm_g, l)
        sink = attn_sink[l]
        ya_l = _attention(qn_l, kn_l, p_l, kn_c, p_c, sink)
        hf_c, fin_f = _lru(p_c, l, 0, zero_state, *lru_p)
        yb_c, fin_b = _lru(p_c, l, 1, zero_state, *lru_p, h_first=hf_c if need_ctx else None)
        hf_l, _ = _lru(p_l, l, 0, fin_f, *lru_p)
        yb_l, _ = _lru(p_l, l, 1, fin_b, *lru_p, h_first=hf_l)
        yc_l = _fourier(p_l)
        yd_l = _conformer(p_l, l, conf_conv_w, conf_conv_b, conf_ln_g, conf_ln_b)
        acc_l = _merge(h_l, (ya_l, yb_l, yc_l, yd_l), gate_w_b, gate_b, w_branch_b, l)
        xl = _mm_res(acc_l, w_out_b, l, xl, mod, l, 2, 0)
        xl = channel_mixer(xl, l, 0)
        if need_ctx:
            ya_c = _ctx_attention(qn_c, kn_c, p_c, sink)
            yc_c = _fourier_direct(p_c)
            yd_c = _conformer(p_c, l, conf_conv_w, conf_conv_b, conf_ln_g, conf_ln_b)
            acc_c = _merge(h_c, (ya_c, yb_c, yc_c, yd_c), gate_w_b, gate_b, w_branch_b, l)
            xc = _mm_res(acc_c, w_out_b, l, xc, mod, l, 2, 1)
            xc = channel_mixer(xc, l, 1)
    return xl[None]
```

```python
import functools
import math

import numpy as np
import jax
import jax.numpy as jnp
from jax import lax
from jax.experimental import pallas as pl
from jax.experimental.pallas import tpu as pltpu

F32 = jnp.float32
BF16 = jnp.bfloat16

HEAD_DIM = 128
N_Q_HEADS = 8
N_KV_HEADS = 2
GROUP = N_Q_HEADS // N_KV_HEADS
ATT_BLOCK = 128
GRID_W = 64
ROPE_THETA = 10000.0
NEG_INF = -1e30
LRU_C = 8.0
LRU_CONV = 4
CONF_KERNEL = 31
FOURIER_GROUP_W = 256
GATE_BLOCK_W = 256
N_EXPERTS = 8
EPS = 1e-6
SLAB = 512
SLAB_Q, SLAB_KV, SLAB_U, SLAB_G, SLAB_F, SLAB_CA, SLAB_CG = 0, 2, 3, 5, 7, 9, 11

V7X_VMEM_LIMIT = 56 * 1024 * 1024
LANES = 128


def _cp(*sem):
    return pltpu.CompilerParams(dimension_semantics=sem, vmem_limit_bytes=V7X_VMEM_LIMIT)


def _sigmoid(x):
    return 1.0 / (1.0 + jnp.exp(-x))


def _dot(a, b):
    return jnp.dot(a, b, preferred_element_type=F32)


def _dot_nt(a, b):
    return lax.dot_general(a, b, (((1,), (1,)), ((), ())), preferred_element_type=F32)


def _split_bf16(x):
    hi = x.astype(BF16)
    lo = (x - hi.astype(F32)).astype(BF16)
    return hi, lo


def _adaln_kernel(c_ref, w_ref, b_ref, o_ref):
    c = c_ref[...]
    s_hi, s_lo = _split_bf16(c * _sigmoid(c))
    w_hi, w_lo = _split_bf16(w_ref[...])
    o_ref[...] = _dot(s_hi, w_hi) + _dot(s_lo, w_hi) + _dot(s_hi, w_lo) + b_ref[...]


def _adaln(c_rows, ada_w, ada_b):
    depth, d, n = ada_w.shape
    tn = 512
    return pl.pallas_call(
        _adaln_kernel,
        grid=(depth, n // tn),
        in_specs=[pl.BlockSpec((8, d), lambda l, j: (0, 0)),
                  pl.BlockSpec((None, d, tn), lambda l, j: (l, 0, j)),
                  pl.BlockSpec((None, 1, tn), lambda l, j: (l, 0, j))],
        out_specs=pl.BlockSpec((None, 8, tn), lambda l, j: (l, 0, j)),
        out_shape=jax.ShapeDtypeStruct((depth, 8, n), F32),
        compiler_params=_cp("arbitrary", "arbitrary"),
        name="adaln",
    )(c_rows, ada_w, ada_b.reshape(depth, 1, n))


def _norm_mod(x_ref, g_ref, sh_ref, sc_ref, row):
    x = x_ref[...]
    y = x * lax.rsqrt(jnp.mean(x * x, axis=-1, keepdims=True) + EPS) * g_ref[...]
    return y * (1.0 + sc_ref[row:row + 1, :]) + sh_ref[row:row + 1, :]


def _norm_kernel(x_ref, g_ref, sh_ref, sc_ref, o_ref, *, row):
    o_ref[...] = _norm_mod(x_ref, g_ref, sh_ref, sc_ref, row).astype(BF16)


def _norm_router_kernel(x_ref, g_ref, sh_ref, sc_ref, rw_ref, rb_ref, o_ref, gates_ref, *, row):
    h = _norm_mod(x_ref, g_ref, sh_ref, sc_ref, row)
    o_ref[...] = h.astype(BF16)
    h_hi, h_lo = _split_bf16(h)
    w_hi, w_lo = _split_bf16(rw_ref[...])
    logits = _dot(h_hi, w_hi) + _dot(h_lo, w_hi) + _dot(h_hi, w_lo) + rb_ref[...]
    lane = lax.broadcasted_iota(jnp.int32, logits.shape, 1)
    logits = jnp.where(lane < N_EXPERTS, logits, -jnp.inf)
    m1 = jnp.max(logits, axis=-1, keepdims=True)
    i1 = jnp.min(jnp.where(logits == m1, lane, LANES), axis=-1, keepdims=True)
    rest = jnp.where(lane == i1, -jnp.inf, logits)
    m2 = jnp.max(rest, axis=-1, keepdims=True)
    i2 = jnp.min(jnp.where(rest == m2, lane, LANES), axis=-1, keepdims=True)
    e2 = jnp.exp(m2 - m1)
    inv = 1.0 / (1.0 + e2)
    gates_ref[...] = jnp.where(lane == i1, inv, 0.0) + jnp.where(lane == i2, e2 * inv, 0.0)


def _norm(x, g_all, mod, layer, chunk, row, router=None):
    m, d = x.shape
    tm = min(512, m)
    in_specs = [pl.BlockSpec((tm, d), lambda i: (i, 0)),
                pl.BlockSpec((None, 1, d), lambda i: (layer, 0, 0)),
                pl.BlockSpec((None, 8, d), lambda i: (layer, 0, chunk)),
                pl.BlockSpec((None, 8, d), lambda i: (layer, 0, chunk + 1))]
    args = [x, g_all.reshape(g_all.shape[0], 1, d), mod, mod]
    h_spec = pl.BlockSpec((tm, d), lambda i: (i, 0))
    h_shape = jax.ShapeDtypeStruct((m, d), BF16)
    if router is None:
        return pl.pallas_call(
            functools.partial(_norm_kernel, row=row), grid=(m // tm,), in_specs=in_specs,
            out_specs=h_spec, out_shape=h_shape, compiler_params=_cp("parallel"), name="norm",
        )(*args)
    rw, rb, li = router
    in_specs += [pl.BlockSpec((None, d, LANES), lambda i: (li, 0, 0)),
                 pl.BlockSpec((None, 1, LANES), lambda i: (li, 0, 0))]
    return pl.pallas_call(
        functools.partial(_norm_router_kernel, row=row), grid=(m // tm,), in_specs=in_specs,
        out_specs=[h_spec, pl.BlockSpec((tm, LANES), lambda i: (i, 0))],
        out_shape=[h_shape, jax.ShapeDtypeStruct((m, LANES), F32)],
        compiler_params=_cp("parallel"), name="norm_router",
    )(*args, rw, rb)


def _mm_kernel(a_ref, w_ref, o_ref):
    o_ref[...] = _dot(a_ref[...], w_ref[...]).astype(o_ref.dtype)


def _in_proj(h, w_all, layer):
    m, d = h.shape
    n = w_all.shape[-1]
    tm = min(2048, m)
    return pl.pallas_call(
        _mm_kernel,
        grid=(m // tm, n // SLAB),
        in_specs=[pl.BlockSpec((tm, d), lambda i, j: (i, 0)),
                  pl.BlockSpec((None, d, SLAB), lambda i, j: (layer, 0, j))],
        out_specs=pl.BlockSpec((None, tm, SLAB), lambda i, j: (j, i, 0)),
        out_shape=jax.ShapeDtypeStruct((n // SLAB, m, SLAB), BF16),
        compiler_params=_cp("parallel", "arbitrary"), name="in_proj",
    )(h, w_all)


def _mm_res_kernel(a_ref, w_ref, x_ref, gate_ref, o_ref, *, row):
    y = _dot(a_ref[...], w_ref[...])
    o_ref[...] = x_ref[...] + gate_ref[row:row + 1, :] * y


def _mm_res(a, w_all, layer, x, mod, mod_layer, chunk, row):
    m, k = a.shape
    d = x.shape[-1]
    tm = min(1024, m)
    tn = 1024
    gate_off = chunk * d // tn
    return pl.pallas_call(
        functools.partial(_mm_res_kernel, row=row),
        grid=(m // tm, d // tn),
        in_specs=[pl.BlockSpec((tm, k), lambda i, j: (i, 0), pipeline_mode=pl.Buffered(1)),
                  pl.BlockSpec((None, k, tn), lambda i, j: (layer, 0, j)),
                  pl.BlockSpec((tm, tn), lambda i, j: (i, j)),
                  pl.BlockSpec((None, 8, tn), lambda i, j: (mod_layer, 0, gate_off + j))],
        out_specs=pl.BlockSpec((tm, tn), lambda i, j: (i, j)),
        out_shape=jax.ShapeDtypeStruct((m, d), F32),
        compiler_params=_cp("parallel", "arbitrary"), name="mm_res",
    )(a, w_all, x, mod)


def _ffn_up_kernel(h_ref, wg_ref, wu_ref, o_ref):
    h = h_ref[...]
    g = _dot(h, wg_ref[...])
    o_ref[...] = ((g * _sigmoid(g)) * _dot(h, wu_ref[...])).astype(o_ref.dtype)


def _moe_up_kernel(h_ref, wgu_ref, gates_ref, o_ref):
    fe = o_ref.shape[-1]
    gu = _dot(h_ref[...], wgu_ref[...])
    g = gu[:, :fe]
    hid = (g * _sigmoid(g)) * gu[:, fe:]
    gates = gates_ref[...]
    lane = lax.broadcasted_iota(jnp.int32, gates.shape, 1)
    gate = jnp.sum(jnp.where(lane == pl.program_id(1), gates, 0.0), axis=-1, keepdims=True)
    o_ref[...] = (hid * gate).astype(o_ref.dtype)


def _ffn_up(h, wg_all, wu_all, li):
    m, d = h.shape
    f = wg_all.shape[-1]
    tm = min(1024, m)
    tn = 512
    w_spec = pl.BlockSpec((None, d, tn), lambda i, j: (li, 0, j))
    return pl.pallas_call(
        _ffn_up_kernel, grid=(m // tm, f // tn),
        in_specs=[pl.BlockSpec((tm, d), lambda i, j: (i, 0)), w_spec, w_spec],
        out_specs=pl.BlockSpec((tm, tn), lambda i, j: (i, j)),
        out_shape=jax.ShapeDtypeStruct((m, f), BF16),
        compiler_params=_cp("parallel", "arbitrary"), name="ffn_up",
    )(h, wg_all, wu_all)


def _moe_up(h, gates, wgu_all, li):
    m, d = h.shape
    n_e, fe = wgu_all.shape[1], wgu_all.shape[-1] // 2
    tm = min(1024, m)
    return pl.pallas_call(
        _moe_up_kernel, grid=(m // tm, n_e),
        in_specs=[pl.BlockSpec((tm, d), lambda i, e: (i, 0)),
                  pl.BlockSpec((None, None, d, 2 * fe), lambda i, e: (li, e, 0, 0)),
                  pl.BlockSpec((tm, LANES), lambda i, e: (i, 0))],
        out_specs=pl.BlockSpec((tm, fe), lambda i, e: (i, e)),
        out_shape=jax.ShapeDtypeStruct((m, n_e * fe), BF16),
        compiler_params=_cp("parallel", "arbitrary"), name="moe_up",
    )(h, wgu_all, gates)


def _rope_tables(length):
    rows = length // GRID_W
    row = jnp.broadcast_to(jnp.arange(rows, dtype=F32)[:, None], (rows, GRID_W)).reshape(-1)
    col = jnp.broadcast_to(jnp.arange(GRID_W, dtype=F32)[None, :], (rows, GRID_W)).reshape(-1)
    n_freq = HEAD_DIM // 4
    inv_freq = ROPE_THETA ** (-jnp.arange(n_freq, dtype=F32) / n_freq)
    ang_r = row[:, None] * inv_freq
    ang_c = col[:, None] * inv_freq
    cr, sr, cc, sc = jnp.cos(ang_r), jnp.sin(ang_r), jnp.cos(ang_c), jnp.sin(ang_c)
    return (jnp.concatenate([cr, cr, cc, cc], axis=-1), jnp.concatenate([-sr, sr, -sc, sc], axis=-1))


def _qk_prep_kernel(q0_ref, q1_ref, kv_ref, cos_ref, sin_ref, g_ref, qn_ref, kn_ref):
    cos = cos_ref[...]
    sin = sin_ref[...]
    t = cos.shape[0]
    lane = lax.broadcasted_iota(jnp.int32, (t, HEAD_DIM), 1)
    low_quarter = (lane & 63) < 32

    def prep(xh, g):
        xh = xh.astype(F32)
        y = xh * lax.rsqrt(jnp.mean(xh * xh, axis=-1, keepdims=True) + EPS) * g
        partner = jnp.where(low_quarter, pltpu.roll(y, HEAD_DIM - 32, 1), pltpu.roll(y, 32, 1))
        return (y * cos + partner * sin).astype(BF16)

    qg = g_ref[0:1, :]
    kg = g_ref[1:2, :]
    for h in range(N_Q_HEADS):
        src = q0_ref if h < GROUP else q1_ref
        c0 = (h % GROUP) * HEAD_DIM
        qh = prep(src[:, c0:c0 + HEAD_DIM], qg)
        for b in range(t // ATT_BLOCK):
            qn_ref[b, h] = qh[b * ATT_BLOCK:(b + 1) * ATT_BLOCK]
    for h in range(N_KV_HEADS):
        c0 = h * HEAD_DIM
        kn_ref[:, c0:c0 + HEAD_DIM] = prep(kv_ref[:, c0:c0 + HEAD_DIM], kg)


def _qk_prep(proj, cos, sin, g_all, layer):
    m = proj.shape[1]
    t = min(512, m)
    tb = t // ATT_BLOCK
    slab = lambda s: pl.BlockSpec((None, t, SLAB), lambda i: (s, i, 0))
    tab = pl.BlockSpec((t, HEAD_DIM), lambda i: (i, 0))
    return pl.pallas_call(
        _qk_prep_kernel, grid=(m // t,),
        in_specs=[slab(SLAB_Q), slab(SLAB_Q + 1),
                  pl.BlockSpec((None, t, N_KV_HEADS * HEAD_DIM), lambda i: (SLAB_KV, i, 0)),
                  tab, tab, pl.BlockSpec((None, 2, HEAD_DIM), lambda i: (layer, 0, 0))],
        out_specs=[pl.BlockSpec((tb, N_Q_HEADS, ATT_BLOCK, HEAD_DIM), lambda i: (i, 0, 0, 0)),
                   pl.BlockSpec((t, N_KV_HEADS * HEAD_DIM), lambda i: (i, 0))],
        out_shape=[jax.ShapeDtypeStruct((m // ATT_BLOCK, N_Q_HEADS, ATT_BLOCK, HEAD_DIM), BF16),
                   jax.ShapeDtypeStruct((m, N_KV_HEADS * HEAD_DIM), BF16)],
        compiler_params=_cp("parallel"), name="qk_prep",
    )(proj, proj, proj, cos, sin, g_all)


def _sink_col(sink_ref, kvh):
    blk = lax.broadcasted_iota(jnp.int32, (GROUP * ATT_BLOCK, 1), 0) >> (ATT_BLOCK.bit_length() - 1)
    col = jnp.full(blk.shape, sink_ref[kvh * GROUP + GROUP - 1], F32)
    for g in range(GROUP - 2, -1, -1):
        col = jnp.where(blk == g, sink_ref[kvh * GROUP + g], col)
    return col


def _softmax_pv(scores, values, sink_col):
    m = sink_col
    for s in scores:
        m = jnp.maximum(m, jnp.max(s, axis=-1, keepdims=True))
    den = jnp.exp(sink_col - m)
    out = None
    for s, v in zip(scores, values):
        e = jnp.exp(s - m)
        den = den + jnp.sum(e, axis=-1, keepdims=True)
        pv = _dot(e.astype(BF16), v)
        out = pv if out is None else out + pv
    return out * (1.0 / den)


def _attn_kernel(sink_ref, q_ref, kp_ref, kc_ref, kn_ref, vp_ref, vc_ref, vn_ref, kx_ref, vx_ref,
                 o_ref, kwin, vwin, *, n_blocks):
    i = pl.program_id(0)
    tb = q_ref.shape[0]
    tq = tb * ATT_BLOCK
    for win, p, c, n in ((kwin, kp_ref, kc_ref, kn_ref), (vwin, vp_ref, vc_ref, vn_ref)):
        win[0:ATT_BLOCK] = p[...]
        win[ATT_BLOCK:ATT_BLOCK + tq] = c[...]
        win[ATT_BLOCK + tq:] = n[...]
    scale = HEAD_DIM ** -0.5
    shape = (GROUP * ATT_BLOCK, 3 * ATT_BLOCK)
    qpos = lax.broadcasted_iota(jnp.int32, shape, 0) & (ATT_BLOCK - 1)
    kpos = lax.broadcasted_iota(jnp.int32, shape, 1)
    band = (kpos >= qpos) & (kpos <= qpos + 2 * ATT_BLOCK)
    for b in range(tb):
        blk = i * tb + b
        lo = jnp.where(blk == 0, ATT_BLOCK, 0)
        hi = jnp.where(blk == n_blocks - 1, 2 * ATT_BLOCK, 3 * ATT_BLOCK)
        valid = band & (kpos >= lo) & (kpos < hi)
        r0 = b * ATT_BLOCK
        for kvh in range(N_KV_HEADS):
            c0 = kvh * HEAD_DIM
            q4 = q_ref[b, kvh * GROUP:(kvh + 1) * GROUP].reshape(GROUP * ATT_BLOCK, HEAD_DIM)
            s_loc = _dot_nt(q4, kwin[r0:r0 + 3 * ATT_BLOCK, c0:c0 + HEAD_DIM]) * scale
            s_loc = jnp.where(valid, s_loc, NEG_INF)
            s_ctx = _dot_nt(q4, kx_ref[:, c0:c0 + HEAD_DIM]) * scale
            o = _softmax_pv((s_loc, s_ctx),
                            (vwin[r0:r0 + 3 * ATT_BLOCK, c0:c0 + HEAD_DIM], vx_ref[:, c0:c0 + HEAD_DIM]),
                            _sink_col(sink_ref, kvh))
            for g in range(GROUP):
                h0 = (kvh * GROUP + g) * HEAD_DIM
                o_ref[r0:r0 + ATT_BLOCK, h0:h0 + HEAD_DIM] = (
                    o[g * ATT_BLOCK:(g + 1) * ATT_BLOCK].astype(o_ref.dtype))


def _attention(qn, kn, proj, kn_ctx, proj_ctx, sink):
    nb = qn.shape[0]
    m = nb * ATT_BLOCK
    cx = kn_ctx.shape[0]
    tb = min(4, nb)
    tq = tb * ATT_BLOCK
    kvw = N_KV_HEADS * HEAD_DIM
    prev = lambda i: jnp.maximum(i * tb - 1, 0)
    nxt = lambda i: jnp.minimum((i + 1) * tb, nb - 1)
    return pl.pallas_call(
        functools.partial(_attn_kernel, n_blocks=nb), grid=(nb // tb,),
        in_specs=[pl.BlockSpec(memory_space=pltpu.SMEM),
                  pl.BlockSpec((tb, N_Q_HEADS, ATT_BLOCK, HEAD_DIM), lambda i: (i, 0, 0, 0)),
                  pl.BlockSpec((ATT_BLOCK, kvw), lambda i: (prev(i), 0)),
                  pl.BlockSpec((tq, kvw), lambda i: (i, 0)),
                  pl.BlockSpec((ATT_BLOCK, kvw), lambda i: (nxt(i), 0)),
                  pl.BlockSpec((None, ATT_BLOCK, kvw), lambda i: (SLAB_KV, prev(i), 1)),
                  pl.BlockSpec((None, tq, kvw), lambda i: (SLAB_KV, i, 1)),
                  pl.BlockSpec((None, ATT_BLOCK, kvw), lambda i: (SLAB_KV, nxt(i), 1)),
                  pl.BlockSpec((cx, kvw), lambda i: (0, 0)),
                  pl.BlockSpec((None, cx, kvw), lambda i: (SLAB_KV, 0, 1))],
        out_specs=pl.BlockSpec((tq, N_Q_HEADS * HEAD_DIM), lambda i: (i, 0)),
        out_shape=jax.ShapeDtypeStruct((m, N_Q_HEADS * HEAD_DIM), BF16),
        scratch_shapes=[pltpu.VMEM((tq + 2 * ATT_BLOCK, kvw), BF16),
                        pltpu.VMEM((tq + 2 * ATT_BLOCK, kvw), BF16)],
        compiler_params=_cp("parallel"), name="attention",
    )(sink, qn, kn, kn, kn, proj, proj, proj, kn_ctx, proj_ctx)


def _ctx_attn_kernel(sink_ref, q_ref, k_ref, v_ref, o_ref):
    scale = HEAD_DIM ** -0.5
    for b in range(q_ref.shape[0]):
        r0 = b * ATT_BLOCK
        for kvh in range(N_KV_HEADS):
            c0 = kvh * HEAD_DIM
            q4 = q_ref[b, kvh * GROUP:(kvh + 1) * GROUP].reshape(GROUP * ATT_BLOCK, HEAD_DIM)
            s = _dot_nt(q4, k_ref[:, c0:c0 + HEAD_DIM]) * scale
            o = _softmax_pv((s,), (v_ref[:, c0:c0 + HEAD_DIM],), _sink_col(sink_ref, kvh))
            for g in range(GROUP):
                h0 = (kvh * GROUP + g) * HEAD_DIM
                o_ref[r0:r0 + ATT_BLOCK, h0:h0 + HEAD_DIM] = (
                    o[g * ATT_BLOCK:(g + 1) * ATT_BLOCK].astype(o_ref.dtype))


def _ctx_attention(qn, kn, proj, sink):
    nb = qn.shape[0]
    cx = nb * ATT_BLOCK
    kvw = N_KV_HEADS * HEAD_DIM
    return pl.pallas_call(
        _ctx_attn_kernel, grid=(1,),
        in_specs=[pl.BlockSpec(memory_space=pltpu.SMEM),
                  pl.BlockSpec((nb, N_Q_HEADS, ATT_BLOCK, HEAD_DIM), lambda i: (0, 0, 0, 0)),
                  pl.BlockSpec((cx, kvw), lambda i: (0, 0)),
                  pl.BlockSpec((None, cx, kvw), lambda i: (SLAB_KV, 0, 1))],
        out_specs=pl.BlockSpec((cx, N_Q_HEADS * HEAD_DIM), lambda i: (0, 0)),
        out_shape=jax.ShapeDtypeStruct((cx, N_Q_HEADS * HEAD_DIM), BF16),
        compiler_params=_cp("arbitrary"), name="ctx_attention",
    )(sink, qn, kn, proj)


def _softplus(z):
    return jnp.maximum(z, 0.0) + jnp.log1p(jnp.exp(-jnp.abs(z)))


def _gelu_tanh(x):
    return 0.5 * x * (1.0 + jnp.tanh(math.sqrt(2.0 / math.pi) * (x + 0.044715 * (x * x * x))))


def _lru_kernel(*refs, reverse, n_t, fuse):
    (u_ref, up_ref, un_ref, cw_ref, cb_ref, lam_ref, wa_ref, ba_ref, wx_ref, bx_ref, h0_ref) = refs[:11]
    if fuse:
        hf_ref, g_ref, o_ref, hfin_ref, ext, a_s, b_s, o_s, carry = refs[11:]
    else:
        o_ref, hfin_ref, ext, a_s, b_s, carry = refs[11:]
        o_s = o_ref
    i = pl.program_id(1)
    tblk = (n_t - 1 - i) if reverse else i
    t = u_ref.shape[0]
    halo = 8
    up = up_ref[...].astype(F32)[up_ref.shape[0] - halo:]
    un = un_ref[...].astype(F32)[:halo]
    ext[0:halo] = jnp.where(tblk == 0, 0.0, up)
    ext[halo:halo + t] = u_ref[...].astype(F32)
    ext[halo + t:] = jnp.where(tblk == n_t - 1, 0.0, un)
    x = cb_ref[...] + cw_ref[0:1, :] * ext[halo - 1:halo - 1 + t]
    for k in range(1, LRU_CONV):
        x = x + cw_ref[k:k + 1, :] * ext[halo - 1 + k:halo - 1 + k + t]
    xb = x.astype(BF16)
    w = HEAD_DIM
    r_pre = jnp.concatenate([_dot(xb[:, n * w:(n + 1) * w], wa_ref[n]) for n in range(wa_ref.shape[0])], axis=-1)
    i_pre = jnp.concatenate([_dot(xb[:, n * w:(n + 1) * w], wx_ref[n]) for n in range(wx_ref.shape[0])], axis=-1)
    r = _sigmoid(r_pre + ba_ref[...])
    ig = _sigmoid(i_pre + bx_ref[...])
    log_a = (-LRU_C * _softplus(-lam_ref[...])) * r
    a = jnp.exp(log_a)
    a_s[...] = a
    b_s[...] = jnp.sqrt(-jnp.tanh(log_a) * (a * a + 1.0)) * (ig * x)

    @pl.when(i == 0)
    def _():
        carry[...] = h0_ref[...]

    sub = 8
    srow = lax.broadcasted_iota(jnp.int32, (sub, a_s.shape[1]), 0)

    def group(s, h):
        r0 = pl.multiple_of((t - sub - s * sub) if reverse else s * sub, sub)
        a_g = a_s[pl.ds(r0, sub), :]
        b_g = b_s[pl.ds(r0, sub), :]
        for dist in (1, 2, 4):
            if reverse:
                shift, ok = sub - dist, srow < sub - dist
            else:
                shift, ok = dist, srow >= dist
            b_g = jnp.where(ok, a_g * pltpu.roll(b_g, shift, 0) + b_g, b_g)
            a_g = jnp.where(ok, a_g * pltpu.roll(a_g, shift, 0), a_g)
        h_g = a_g * h + b_g
        o_s[pl.ds(r0, sub), :] = h_g
        return h_g[0:1] if reverse else h_g[sub - 1:sub]

    h = lax.fori_loop(0, t // sub, group, carry[0:1, :], unroll=4)
    carry[...] = jnp.broadcast_to(h, carry.shape)
    hfin_ref[...] = jnp.broadcast_to(h, hfin_ref.shape)
    if fuse:
        o_ref[...] = ((hf_ref[...] + o_s[...]) * _gelu_tanh(g_ref[...].astype(F32))).astype(o_ref.dtype)


def _lru(proj, layer, direction, h0, conv_w, conv_b, lam, wa, ba, wx, bx, h_first=None):
    m = proj.shape[1]
    width = conv_w.shape[-1]
    n_slab = width // SLAB
    t = min(1024, m)
    n_t = m // t
    hb = 16
    reverse = direction == 1
    fuse = h_first is not None
    tmap = (lambda i: n_t - 1 - i) if reverse else (lambda i: i)
    n_gate_blk = SLAB // HEAD_DIM
    vec = lambda a: a.reshape(a.shape[:-1] + (1, width))
    dvec = pl.BlockSpec((None, None, 1, SLAB), lambda s, i: (layer, direction, 0, s))
    dmat = pl.BlockSpec((None, None, n_gate_blk, HEAD_DIM, HEAD_DIM), lambda s, i: (layer, direction, s, 0, 0))
    in_specs = [
        pl.BlockSpec((None, t, SLAB), lambda s, i: (SLAB_U + s, tmap(i), 0)),
        pl.BlockSpec((None, hb, SLAB), lambda s, i: (SLAB_U + s, jnp.maximum(tmap(i) * (t // hb) - 1, 0), 0)),
        pl.BlockSpec((None, hb, SLAB), lambda s, i: (SLAB_U + s, jnp.minimum((tmap(i) + 1) * (t // hb), m // hb - 1), 0)),
        pl.BlockSpec((None, LRU_CONV, SLAB), lambda s, i: (layer, 0, s)),
        pl.BlockSpec((None, 1, SLAB), lambda s, i: (layer, 0, s)),
        dvec, dmat, dvec, dmat, dvec,
        pl.BlockSpec((8, SLAB), lambda s, i: (0, s)),
    ]
    args = [proj, proj, proj, conv_w, vec(conv_b), vec(lam), wa, vec(ba), wx, vec(bx), h0]
    seq_spec = pl.BlockSpec((t, SLAB), lambda s, i: (tmap(i), s))
    scratch = [pltpu.VMEM((t + 16, SLAB), F32), pltpu.VMEM((t, SLAB), F32), pltpu.VMEM((t, SLAB), F32)]
    if fuse:
        in_specs += [seq_spec, pl.BlockSpec((None, t, SLAB), lambda s, i: (SLAB_G + s, tmap(i), 0))]
        args += [h_first, proj]
        scratch.append(pltpu.VMEM((t, SLAB), F32))
    scratch.append(pltpu.VMEM((8, SLAB), F32))
    return pl.pallas_call(
        functools.partial(_lru_kernel, reverse=reverse, n_t=n_t, fuse=fuse),
        grid=(n_slab, n_t), in_specs=in_specs,
        out_specs=[seq_spec, pl.BlockSpec((8, SLAB), lambda s, i: (0, s))],
        out_shape=[jax.ShapeDtypeStruct((m, width), BF16 if fuse else F32),
                   jax.ShapeDtypeStruct((8, width), F32)],
        scratch_shapes=scratch,
        compiler_params=_cp("arbitrary", "arbitrary"), name="lru",
    )(*args)


def _dft_tables(n):
    ang = 2.0 * np.pi * np.outer(np.arange(n), np.arange(n)) / n
    return np.cos(ang), np.sin(ang)


def _fourier1_kernel(x_ref, f_ref, twc_ref, tws_ref, br_ref, bi_ref):
    n1 = x_ref.shape[0]
    f = f_ref[...]
    for j in range(x_ref.shape[1]):
        a = _dot(f, x_ref[:, j, :])
        ar, ai = a[:n1], a[n1:]
        tc, ts = twc_ref[j], tws_ref[j]
        br_ref[:, j, :] = (ar * tc + ai * ts).astype(br_ref.dtype)
        bi_ref[:, j, :] = (ai * tc - ar * ts).astype(bi_ref.dtype)


def _fourier2_kernel(br_ref, bi_ref, c_ref, s_ref, cc_ref, sc_ref, o_ref, *, scale):
    n_slab, kb = br_ref.shape[0], br_ref.shape[1]
    c, s = c_ref[...], s_ref[...]
    gw = FOURIER_GROUP_W
    for j in range(kb):
        for sl in range(n_slab):
            br, bi = br_ref[sl, j], bi_ref[sl, j]
            pr = (_dot(c, br) + _dot(s, bi)).astype(BF16)
            pi = (_dot(c, bi) - _dot(s, br)).astype(BF16)
            for g in range(SLAB // gw):
                y = _dot(pr[:, g * gw:(g + 1) * gw], cc_ref[...]) + _dot(pi[:, g * gw:(g + 1) * gw], sc_ref[...])
                c0 = sl * SLAB + g * gw
                o_ref[:, j, c0:c0 + gw] = (y * scale).astype(o_ref.dtype)


def _fourier(proj):
    n_seg, m, _ = proj.shape
    n1 = 1 << ((m.bit_length() - 1) // 2)
    n2 = m // n1
    assert n1 * n2 == m and n1 % 16 == 0 and n2 % 16 == 0
    n_slab = 2
    width = n_slab * SLAB
    c1, s1 = _dft_tables(n1)
    f1 = jnp.asarray(np.concatenate([c1, -s1], axis=0), BF16)
    tau = 2.0 * np.pi * np.outer(np.arange(n2), np.arange(n1)) / m
    twc = jnp.asarray(np.cos(tau)[:, :, None], F32)
    tws = jnp.asarray(np.sin(tau)[:, :, None], F32)
    nl2 = 16
    b_spec = pl.BlockSpec((None, n1, nl2, SLAB), lambda s, j: (s, 0, j, 0))
    b_shape = jax.ShapeDtypeStruct((n_slab, n1, n2, SLAB), BF16)
    tw_spec = pl.BlockSpec((nl2, n1, 1), lambda s, j: (j, 0, 0))
    br, bi = pl.pallas_call(
        _fourier1_kernel, grid=(n_slab, n2 // nl2),
        in_specs=[pl.BlockSpec((None, n1, nl2, SLAB), lambda s, j: (SLAB_F + s, 0, j, 0)),
                  pl.BlockSpec((2 * n1, n1), lambda s, j: (0, 0)), tw_spec, tw_spec],
        out_specs=[b_spec, b_spec], out_shape=[b_shape, b_shape],
        compiler_params=_cp("parallel", "arbitrary"), name="fourier1",
    )(proj.reshape(n_seg, n1, n2, SLAB), f1, twc, tws)

    c2, s2 = _dft_tables(n2)
    cc, sc = _dft_tables(FOURIER_GROUP_W)
    kb = 16
    in4 = pl.BlockSpec((n_slab, kb, n2, SLAB), lambda k: (0, k, 0, 0))
    full = lambda n: pl.BlockSpec((n, n), lambda k: (0, 0))
    out = pl.pallas_call(
        functools.partial(_fourier2_kernel, scale=1.0 / math.sqrt(m * FOURIER_GROUP_W)),
        grid=(n1 // kb,),
        in_specs=[in4, in4, full(n2), full(n2), full(FOURIER_GROUP_W), full(FOURIER_GROUP_W)],
        out_specs=pl.BlockSpec((n2, kb, width), lambda k: (0, k, 0)),
        out_shape=jax.ShapeDtypeStruct((n2, n1, width), BF16),
        compiler_params=_cp("parallel"), name="fourier2",
    )(br, bi, jnp.asarray(c2, BF16), jnp.asarray(s2, BF16), jnp.asarray(cc, BF16), jnp.asarray(sc, BF16))
    return out.reshape(m, width)


def _fourier_direct_kernel(u0_ref, u1_ref, cl_ref, sl_ref, cc_ref, sc_ref, o_ref, *, scale):
    gw = FOURIER_GROUP_W
    for sl, u_ref in enumerate((u0_ref, u1_ref)):
        u = u_ref[...]
        pr = _dot(cl_ref[...], u).astype(BF16)
        pi = (-_dot(sl_ref[...], u)).astype(BF16)
        for g in range(SLAB // gw):
            y = _dot(pr[:, g * gw:(g + 1) * gw], cc_ref[...]) + _dot(pi[:, g * gw:(g + 1) * gw], sc_ref[...])
            c0 = sl * SLAB + g * gw
            o_ref[:, c0:c0 + gw] = (y * scale).astype(o_ref.dtype)


def _fourier_direct(proj):
    m = proj.shape[1]
    cl, sl = _dft_tables(m)
    cc, sc = _dft_tables(FOURIER_GROUP_W)
    full = lambda n: pl.BlockSpec((n, n), lambda i: (0, 0))
    return pl.pallas_call(
        functools.partial(_fourier_direct_kernel, scale=1.0 / math.sqrt(m * FOURIER_GROUP_W)),
        grid=(1,),
        in_specs=[pl.BlockSpec((None, m, SLAB), lambda i: (SLAB_F, 0, 0)),
                  pl.BlockSpec((None, m, SLAB), lambda i: (SLAB_F + 1, 0, 0)),
                  full(m), full(m), full(FOURIER_GROUP_W), full(FOURIER_GROUP_W)],
        out_specs=pl.BlockSpec((m, 2 * SLAB), lambda i: (0, 0)),
        out_shape=jax.ShapeDtypeStruct((m, 2 * SLAB), BF16),
        compiler_params=_cp("arbitrary"), name="fourier_direct",
    )(proj, proj, jnp.asarray(cl, BF16), jnp.asarray(sl, BF16), jnp.asarray(cc, BF16), jnp.asarray(sc, BF16))


CONF_HALO = 16
CONF_ROWS = 32


def _conformer_kernel(*refs, n_t):
    cur = refs[0:4]
    prv = refs[4:8]
    nxt = refs[8:12]
    cw_ref, cb_ref, lg_ref, lb_ref, o_ref, ext, shifted, conv = refs[12:]
    i = pl.program_id(0)
    t = cur[0].shape[0]
    width = 2 * SLAB

    def glu(parts, keep):
        for sl in range(2):
            a = parts[sl][...].astype(F32)
            g = parts[2 + sl][...].astype(F32)
            yield sl, jnp.where(keep, a * _sigmoid(g), 0.0)

    for sl, v in glu(prv, i > 0):
        ext[0:CONF_HALO, sl * SLAB:(sl + 1) * SLAB] = v
    for sl, v in glu(cur, True):
        ext[CONF_HALO:CONF_HALO + t, sl * SLAB:(sl + 1) * SLAB] = v
    for sl, v in glu(nxt, i < n_t - 1):
        ext[CONF_HALO + t:, sl * SLAB:(sl + 1) * SLAB] = v
    rows = shifted.shape[1]
    for j in range(8):
        shifted[j] = ext[j:j + rows]

    n_ct = width // LANES

    def chunk(rc, _):
        r0 = pl.multiple_of(rc * CONF_ROWS, CONF_ROWS)
        for c in range(n_ct):
            cs = slice(c * LANES, (c + 1) * LANES)
            acc = [jnp.broadcast_to(cb_ref[:, cs], (8, LANES))] * (CONF_ROWS // 8)
            for k in range(CONF_KERNEL):
                off = k + 1
                wk = cw_ref[k, :, cs]
                for r in range(CONF_ROWS // 8):
                    acc[r] = acc[r] + wk * shifted[off % 8, pl.ds(r0 + (off - off % 8) + 8 * r, 8), cs]
            for r in range(CONF_ROWS // 8):
                conv[pl.ds(r0 + 8 * r, 8), cs] = acc[r]
        return 0

    lax.fori_loop(0, t // CONF_ROWS, chunk, 0)
    v = conv[...]
    d = v - jnp.mean(v, axis=-1, keepdims=True)
    y = d * lax.rsqrt(jnp.mean(d * d, axis=-1, keepdims=True) + EPS) * lg_ref[...] + lb_ref[...]
    o_ref[...] = (y * _sigmoid(y)).astype(o_ref.dtype)


def _conformer(proj, layer, conv_w_rep, conv_b, ln_g, ln_b):
    m = proj.shape[1]
    width = conv_w_rep.shape[-1]
    t = min(256, m)
    n_t = m // t
    hb = CONF_HALO
    cur = lambda s: pl.BlockSpec((None, t, SLAB), lambda i: (s, i, 0))
    prv = lambda s: pl.BlockSpec((None, hb, SLAB), lambda i: (s, jnp.maximum(i * (t // hb) - 1, 0), 0))
    nxt = lambda s: pl.BlockSpec((None, hb, SLAB), lambda i: (s, jnp.minimum((i + 1) * (t // hb), m // hb - 1), 0))
    slabs = (SLAB_CA, SLAB_CA + 1, SLAB_CG, SLAB_CG + 1)
    vec = lambda a: a.reshape(a.shape[0], 1, width)
    vspec = pl.BlockSpec((None, 1, width), lambda i: (layer, 0, 0))
    return pl.pallas_call(
        functools.partial(_conformer_kernel, n_t=n_t), grid=(n_t,),
        in_specs=[cur(s) for s in slabs] + [prv(s) for s in slabs] + [nxt(s) for s in slabs]
        + [pl.BlockSpec((None, CONF_KERNEL, 8, width), lambda i: (layer, 0, 0, 0)), vspec, vspec, vspec],
        out_specs=pl.BlockSpec((t, width), lambda i: (i, 0)),
        out_shape=jax.ShapeDtypeStruct((m, width), BF16),
        scratch_shapes=[pltpu.VMEM((t + 2 * hb, width), F32), pltpu.VMEM((8, t + 2 * hb - 8, width), F32),
                        pltpu.VMEM((t, width), F32)],
        compiler_params=_cp("parallel"), name="conformer",
    )(*([proj] * 12), conv_w_rep, vec(conv_b), vec(ln_g), vec(ln_b))


def _merge_kernel(h_ref, gw_ref, gb_ref, ya_ref, yb_ref, yc_ref, yd_ref, wb_ref, o_ref):
    gw = GATE_BLOCK_W
    acc = None
    for b, y_ref in enumerate((ya_ref, yb_ref, yc_ref, yd_ref)):
        pre = jnp.concatenate([_dot(h_ref[:, n * gw:(n + 1) * gw], gw_ref[b, n])
                               for n in range(gw_ref.shape[1])], axis=-1)
        term = _sigmoid(pre + gb_ref[b:b + 1, :]) * _dot(y_ref[...], wb_ref[b])
        acc = term if acc is None else acc + term
    o_ref[...] = acc.astype(o_ref.dtype)


def _merge(h, ys, gate_w, gate_b, w_branch, layer):
    m, d = h.shape
    bw = ys[0].shape[-1]
    nbr = len(ys)
    tm = min(1024, m)
    gpb = 2
    tn = gpb * GATE_BLOCK_W
    y_spec = pl.BlockSpec((tm, bw), lambda i, n: (i, 0))
    return pl.pallas_call(
        _merge_kernel, grid=(m // tm, d // tn),
        in_specs=[pl.BlockSpec((tm, tn), lambda i, n: (i, n)),
                  pl.BlockSpec((None, nbr, gpb, GATE_BLOCK_W, GATE_BLOCK_W), lambda i, n: (layer, 0, n, 0, 0)),
                  pl.BlockSpec((None, nbr, tn), lambda i, n: (layer, 0, n)),
                  y_spec, y_spec, y_spec, y_spec,
                  pl.BlockSpec((None, nbr, bw, tn), lambda i, n: (layer, 0, 0, n))],
        out_specs=pl.BlockSpec((tm, tn), lambda i, n: (i, n)),
        out_shape=jax.ShapeDtypeStruct((m, d), BF16),
        compiler_params=_cp("parallel", "arbitrary"), name="merge",
    )(h, gate_w, gate_b, *ys, w_branch)


def kernel(x, c, ctx, c_ctx, ada_w, ada_b, norm_mix_g, norm_ffn_g, w_in, qk_norm_g, attn_sink,
           lru_conv_w, lru_conv_b, lru_lambda, lru_wa, lru_ba, lru_wx, lru_bx,
           conf_conv_w, conf_conv_b, conf_ln_g, conf_ln_b, w_branch, gate_w, gate_b, w_out,
           ffn_w_gate, ffn_w_up, ffn_w_down, router_w, router_b, moe_w_gate, moe_w_up, moe_w_down):
    assert x.shape[0] == 1 and ctx.shape[0] == 1 and c.shape[0] == 1
    depth = ada_w.shape[0]
    seq, d = x.shape[1], x.shape[2]
    cx = ctx.shape[1]
    xl = x[0]
    xc = ctx[0]

    w_in_b, w_branch_b, gate_w_b, w_out_b = (a.astype(BF16) for a in (w_in, w_branch, gate_w, w_out))
    ffn_g_b, ffn_u_b, ffn_d_b = (a.astype(BF16) for a in (ffn_w_gate, ffn_w_up, ffn_w_down))
    moe_gu_b = jnp.concatenate([moe_w_gate.astype(BF16), moe_w_up.astype(BF16)], axis=-1)
    moe_d_b = moe_w_down.astype(BF16).reshape(moe_w_down.shape[0], -1, d)
    lru_wa_b, lru_wx_b = lru_wa.astype(BF16), lru_wx.astype(BF16)
    conf_w_rep = jnp.broadcast_to(conf_conv_w[:, :, None, :], conf_conv_w.shape[:2] + (8,) + conf_conv_w.shape[2:])
    router_w_p = jnp.pad(router_w, ((0, 0), (0, 0), (0, LANES - N_EXPERTS)))
    router_b_p = jnp.pad(router_b, ((0, 0), (0, LANES - N_EXPERTS)))[:, None, :]

    c_rows = jnp.concatenate([c, c_ctx[None, :], jnp.zeros((6, d), F32)], axis=0)
    mod = _adaln(c_rows, ada_w, ada_b)

    cos_l, sin_l = _rope_tables(seq)
    cos_c, sin_c = jnp.ones((cx, HEAD_DIM), F32), jnp.zeros((cx, HEAD_DIM), F32)
    zero_state = jnp.zeros((8, lru_conv_w.shape[-1]), F32)
    lru_p = (lru_conv_w, lru_conv_b, lru_lambda, lru_wa_b, lru_ba, lru_wx_b, lru_bx)

    def channel_mixer(xs, l, row):
        li = l // 2
        if l % 2 == 0:
            h = _norm(xs, norm_ffn_g, mod, l, 3, row)
            hid = _ffn_up(h, ffn_g_b, ffn_u_b, li)
            return _mm_res(hid, ffn_d_b, li, xs, mod, l, 5, row)
        h, gates = _norm(xs, norm_ffn_g, mod, l, 3, row, router=(router_w_p, router_b_p, li))
        hid = _moe_up(h, gates, moe_gu_b, li)
        return _mm_res(hid, moe_d_b, li, xs, mod, l, 5, row)

    for l in range(depth):
        need_ctx = l < depth - 1
        h_l = _norm(xl, norm_mix_g, mod, l, 0, 0)
        h_c = _norm(xc, norm_mix_g, mod, l, 0, 1)
        p_l = _in_proj(h_l, w_in_b, l)
        p_c = _in_proj(h_c, w_in_b, l)
        qn_l, kn_l = _qk_prep(p_l, cos_l, sin_l, qk_norm_g, l)
        qn_c, kn_c = _qk_prep(p_c, cos_c, sin_c, qk_norm_g, l)
        sink = attn_sink[l]
        ya_l = _attention(qn_l, kn_l, p_l, kn_c, p_c, sink)
        hf_c, fin_f = _lru(p_c, l, 0, zero_state, *lru_p)
        yb_c, fin_b = _lru(p_c, l, 1, zero_state, *lru_p, h_first=hf_c if need_ctx else None)
        hf_l, _ = _lru(p_l, l, 0, fin_f, *lru_p)
        yb_l, _ = _lru(p_l, l, 1, fin_b, *lru_p, h_first=hf_l)
        yc_l = _fourier(p_l)
        yd_l = _conformer(p_l, l, conf_w_rep, conf_conv_b, conf_ln_g, conf_ln_b)
        acc_l = _merge(h_l, (ya_l, yb_l, yc_l, yd_l), gate_w_b, gate_b, w_branch_b, l)
        xl = _mm_res(acc_l, w_out_b, l, xl, mod, l, 2, 0)
        xl = channel_mixer(xl, l, 0)
        if need_ctx:
            ya_c = _ctx_attention(qn_c, kn_c, p_c, sink)
            yc_c = _fourier_direct(p_c)
            yd_c = _conformer(p_c, l, conf_w_rep, conf_conv_b, conf_ln_g, conf_ln_b)
            acc_c = _merge(h_c, (ya_c, yb_c, yc_c, yd_c), gate_w_b, gate_b, w_branch_b, l)
            xc = _mm_res(acc_c, w_out_b, l, xc, mod, l, 2, 1)
            xc = channel_mixer(xc, l, 1)
    return xl[None]
```

```python
import functools
import math

import numpy as np
import jax
import jax.numpy as jnp
from jax import lax
from jax.experimental import pallas as pl
from jax.experimental.pallas import tpu as pltpu

F32 = jnp.float32
BF16 = jnp.bfloat16

HEAD_DIM = 128
N_Q_HEADS = 8
N_KV_HEADS = 2
GROUP = N_Q_HEADS // N_KV_HEADS
ATT_BLOCK = 128
GRID_W = 64
ROPE_THETA = 10000.0
NEG_INF = -1e30
LRU_C = 8.0
LRU_CONV = 4
CONF_KERNEL = 31
FOURIER_GROUP_W = 256
GATE_BLOCK_W = 256
N_EXPERTS = 8
EPS = 1e-6
SLAB = 512
SLAB_Q, SLAB_KV, SLAB_U, SLAB_G, SLAB_F, SLAB_CA, SLAB_CG = 0, 2, 3, 5, 7, 9, 11

V7X_VMEM_LIMIT = 56 * 1024 * 1024
LANES = 128


def _cp(*sem):
    return pltpu.CompilerParams(dimension_semantics=sem, vmem_limit_bytes=V7X_VMEM_LIMIT)


def _sigmoid(x):
    return 1.0 / (1.0 + jnp.exp(-x))


def _dot(a, b):
    return jnp.dot(a, b, preferred_element_type=F32)


def _dot_nt(a, b):
    return lax.dot_general(a, b, (((1,), (1,)), ((), ())), preferred_element_type=F32)


def _split_bf16(x):
    hi = x.astype(BF16)
    lo = (x - hi.astype(F32)).astype(BF16)
    return hi, lo


def _adaln_kernel(c_ref, w_ref, b_ref, o_ref):
    c = c_ref[...]
    s_hi, s_lo = _split_bf16(c * _sigmoid(c))
    w_hi, w_lo = _split_bf16(w_ref[...])
    o_ref[...] = _dot(s_hi, w_hi) + _dot(s_lo, w_hi) + _dot(s_hi, w_lo) + b_ref[...]


def _adaln(c_rows, ada_w, ada_b):
    depth, d, n = ada_w.shape
    tn = 512
    return pl.pallas_call(
        _adaln_kernel,
        grid=(depth, n // tn),
        in_specs=[pl.BlockSpec((8, d), lambda l, j: (0, 0)),
                  pl.BlockSpec((None, d, tn), lambda l, j: (l, 0, j)),
                  pl.BlockSpec((None, 1, tn), lambda l, j: (l, 0, j))],
        out_specs=pl.BlockSpec((None, 8, tn), lambda l, j: (l, 0, j)),
        out_shape=jax.ShapeDtypeStruct((depth, 8, n), F32),
        compiler_params=_cp("arbitrary", "arbitrary"),
        name="adaln",
    )(c_rows, ada_w, ada_b.reshape(depth, 1, n))


def _norm_mod(x_ref, g_ref, sh_ref, sc_ref, row):
    x = x_ref[...]
    y = x * lax.rsqrt(jnp.mean(x * x, axis=-1, keepdims=True) + EPS) * g_ref[...]
    return y * (1.0 + sc_ref[row:row + 1, :]) + sh_ref[row:row + 1, :]


def _norm_kernel(x_ref, g_ref, sh_ref, sc_ref, o_ref, *, row):
    o_ref[...] = _norm_mod(x_ref, g_ref, sh_ref, sc_ref, row).astype(BF16)


def _norm_router_kernel(x_ref, g_ref, sh_ref, sc_ref, rw_ref, rb_ref, o_ref, gates_ref, *, row):
    h = _norm_mod(x_ref, g_ref, sh_ref, sc_ref, row)
    o_ref[...] = h.astype(BF16)
    h_hi, h_lo = _split_bf16(h)
    w_hi, w_lo = _split_bf16(rw_ref[...])
    logits = _dot(h_hi, w_hi) + _dot(h_lo, w_hi) + _dot(h_hi, w_lo) + rb_ref[...]
    lane = lax.broadcasted_iota(jnp.int32, logits.shape, 1)
    logits = jnp.where(lane < N_EXPERTS, logits, -jnp.inf)
    m1 = jnp.max(logits, axis=-1, keepdims=True)
    i1 = jnp.min(jnp.where(logits == m1, lane, LANES), axis=-1, keepdims=True)
    rest = jnp.where(lane == i1, -jnp.inf, logits)
    m2 = jnp.max(rest, axis=-1, keepdims=True)
    i2 = jnp.min(jnp.where(rest == m2, lane, LANES), axis=-1, keepdims=True)
    e2 = jnp.exp(m2 - m1)
    inv = 1.0 / (1.0 + e2)
    gates_ref[...] = jnp.where(lane == i1, inv, 0.0) + jnp.where(lane == i2, e2 * inv, 0.0)


def _norm(x, g_all, mod, layer, chunk, row, router=None):
    m, d = x.shape
    tm = min(512, m)
    in_specs = [pl.BlockSpec((tm, d), lambda i: (i, 0)),
                pl.BlockSpec((None, 1, d), lambda i: (layer, 0, 0)),
                pl.BlockSpec((None, 8, d), lambda i: (layer, 0, chunk)),
                pl.BlockSpec((None, 8, d), lambda i: (layer, 0, chunk + 1))]
    args = [x, g_all.reshape(g_all.shape[0], 1, d), mod, mod]
    h_spec = pl.BlockSpec((tm, d), lambda i: (i, 0))
    h_shape = jax.ShapeDtypeStruct((m, d), BF16)
    if router is None:
        return pl.pallas_call(
            functools.partial(_norm_kernel, row=row), grid=(m // tm,), in_specs=in_specs,
            out_specs=h_spec, out_shape=h_shape, compiler_params=_cp("parallel"), name="norm",
        )(*args)
    rw, rb, li = router
    in_specs += [pl.BlockSpec((None, d, LANES), lambda i: (li, 0, 0)),
                 pl.BlockSpec((None, 1, LANES), lambda i: (li, 0, 0))]
    return pl.pallas_call(
        functools.partial(_norm_router_kernel, row=row), grid=(m // tm,), in_specs=in_specs,
        out_specs=[h_spec, pl.BlockSpec((tm, LANES), lambda i: (i, 0))],
        out_shape=[h_shape, jax.ShapeDtypeStruct((m, LANES), F32)],
        compiler_params=_cp("parallel"), name="norm_router",
    )(*args, rw, rb)


def _mm_kernel(a_ref, w_ref, o_ref):
    o_ref[...] = _dot(a_ref[...], w_ref[...]).astype(o_ref.dtype)


def _in_proj(h, w_all, layer):
    m, d = h.shape
    n = w_all.shape[-1]
    tm = min(2048, m)
    return pl.pallas_call(
        _mm_kernel,
        grid=(m // tm, n // SLAB),
        in_specs=[pl.BlockSpec((tm, d), lambda i, j: (i, 0)),
                  pl.BlockSpec((None, d, SLAB), lambda i, j: (layer, 0, j))],
        out_specs=pl.BlockSpec((None, tm, SLAB), lambda i, j: (j, i, 0)),
        out_shape=jax.ShapeDtypeStruct((n // SLAB, m, SLAB), BF16),
        compiler_params=_cp("parallel", "arbitrary"), name="in_proj",
    )(h, w_all)


def _mm_res_kernel(a_ref, w_ref, x_ref, gate_ref, o_ref, *, row):
    y = _dot(a_ref[...], w_ref[...])
    o_ref[...] = x_ref[...] + gate_ref[row:row + 1, :] * y


def _mm_res(a, w_all, layer, x, mod, mod_layer, chunk, row):
    m, k = a.shape
    d = x.shape[-1]
    tm = min(1024, m)
    tn = 512
    gate_off = chunk * d // tn
    return pl.pallas_call(
        functools.partial(_mm_res_kernel, row=row),
        grid=(m // tm, d // tn),
        in_specs=[pl.BlockSpec((tm, k), lambda i, j: (i, 0)),
                  pl.BlockSpec((None, k, tn), lambda i, j: (layer, 0, j)),
                  pl.BlockSpec((tm, tn), lambda i, j: (i, j)),
                  pl.BlockSpec((None, 8, tn), lambda i, j: (mod_layer, 0, gate_off + j))],
        out_specs=pl.BlockSpec((tm, tn), lambda i, j: (i, j)),
        out_shape=jax.ShapeDtypeStruct((m, d), F32),
        compiler_params=_cp("parallel", "arbitrary"), name="mm_res",
    )(a, w_all, x, mod)


def _ffn_up_kernel(h_ref, wg_ref, wu_ref, o_ref):
    h = h_ref[...]
    g = _dot(h, wg_ref[...])
    o_ref[...] = ((g * _sigmoid(g)) * _dot(h, wu_ref[...])).astype(o_ref.dtype)


def _moe_up_kernel(h_ref, wgu_ref, gates_ref, o_ref):
    fe = o_ref.shape[-1]
    gu = _dot(h_ref[...], wgu_ref[...])
    g = gu[:, :fe]
    hid = (g * _sigmoid(g)) * gu[:, fe:]
    gates = gates_ref[...]
    lane = lax.broadcasted_iota(jnp.int32, gates.shape, 1)
    gate = jnp.sum(jnp.where(lane == pl.program_id(1), gates, 0.0), axis=-1, keepdims=True)
    o_ref[...] = (hid * gate).astype(o_ref.dtype)


def _ffn_up(h, wg_all, wu_all, li):
    m, d = h.shape
    f = wg_all.shape[-1]
    tm = min(1024, m)
    tn = 512
    w_spec = pl.BlockSpec((None, d, tn), lambda i, j: (li, 0, j))
    return pl.pallas_call(
        _ffn_up_kernel, grid=(m // tm, f // tn),
        in_specs=[pl.BlockSpec((tm, d), lambda i, j: (i, 0)), w_spec, w_spec],
        out_specs=pl.BlockSpec((tm, tn), lambda i, j: (i, j)),
        out_shape=jax.ShapeDtypeStruct((m, f), BF16),
        compiler_params=_cp("parallel", "arbitrary"), name="ffn_up",
    )(h, wg_all, wu_all)


def _moe_up(h, gates, wgu_all, li):
    m, d = h.shape
    n_e, fe = wgu_all.shape[1], wgu_all.shape[-1] // 2
    tm = min(1024, m)
    return pl.pallas_call(
        _moe_up_kernel, grid=(m // tm, n_e),
        in_specs=[pl.BlockSpec((tm, d), lambda i, e: (i, 0)),
                  pl.BlockSpec((None, None, d, 2 * fe), lambda i, e: (li, e, 0, 0)),
                  pl.BlockSpec((tm, LANES), lambda i, e: (i, 0))],
        out_specs=pl.BlockSpec((tm, fe), lambda i, e: (i, e)),
        out_shape=jax.ShapeDtypeStruct((m, n_e * fe), BF16),
        compiler_params=_cp("parallel", "arbitrary"), name="moe_up",
    )(h, wgu_all, gates)


def _rope_tables(length):
    rows = length // GRID_W
    row = jnp.broadcast_to(jnp.arange(rows, dtype=F32)[:, None], (rows, GRID_W)).reshape(-1)
    col = jnp.broadcast_to(jnp.arange(GRID_W, dtype=F32)[None, :], (rows, GRID_W)).reshape(-1)
    n_freq = HEAD_DIM // 4
    inv_freq = ROPE_THETA ** (-jnp.arange(n_freq, dtype=F32) / n_freq)
    ang_r = row[:, None] * inv_freq
    ang_c = col[:, None] * inv_freq
    cr, sr, cc, sc = jnp.cos(ang_r), jnp.sin(ang_r), jnp.cos(ang_c), jnp.sin(ang_c)
    return (jnp.concatenate([cr, cr, cc, cc], axis=-1), jnp.concatenate([-sr, sr, -sc, sc], axis=-1))


def _qk_prep_kernel(q0_ref, q1_ref, kv_ref, cos_ref, sin_ref, g_ref, qn_ref, kn_ref):
    cos = cos_ref[...]
    sin = sin_ref[...]
    t = cos.shape[0]
    lane = lax.broadcasted_iota(jnp.int32, (t, HEAD_DIM), 1)
    low_quarter = (lane & 63) < 32

    def prep(xh, g):
        xh = xh.astype(F32)
        y = xh * lax.rsqrt(jnp.mean(xh * xh, axis=-1, keepdims=True) + EPS) * g
        partner = jnp.where(low_quarter, pltpu.roll(y, HEAD_DIM - 32, 1), pltpu.roll(y, 32, 1))
        return (y * cos + partner * sin).astype(BF16)

    qg = g_ref[0:1, :]
    kg = g_ref[1:2, :]
    for h in range(N_Q_HEADS):
        src = q0_ref if h < GROUP else q1_ref
        c0 = (h % GROUP) * HEAD_DIM
        qh = prep(src[:, c0:c0 + HEAD_DIM], qg)
        for b in range(t // ATT_BLOCK):
            qn_ref[b, h] = qh[b * ATT_BLOCK:(b + 1) * ATT_BLOCK]
    for h in range(N_KV_HEADS):
        c0 = h * HEAD_DIM
        kn_ref[:, c0:c0 + HEAD_DIM] = prep(kv_ref[:, c0:c0 + HEAD_DIM], kg)


def _qk_prep(proj, cos, sin, g_all, layer):
    m = proj.shape[1]
    t = min(512, m)
    tb = t // ATT_BLOCK
    slab = lambda s: pl.BlockSpec((None, t, SLAB), lambda i: (s, i, 0))
    tab = pl.BlockSpec((t, HEAD_DIM), lambda i: (i, 0))
    return pl.pallas_call(
        _qk_prep_kernel, grid=(m // t,),
        in_specs=[slab(SLAB_Q), slab(SLAB_Q + 1),
                  pl.BlockSpec((None, t, N_KV_HEADS * HEAD_DIM), lambda i: (SLAB_KV, i, 0)),
                  tab, tab, pl.BlockSpec((None, 2, HEAD_DIM), lambda i: (layer, 0, 0))],
        out_specs=[pl.BlockSpec((tb, N_Q_HEADS, ATT_BLOCK, HEAD_DIM), lambda i: (i, 0, 0, 0)),
                   pl.BlockSpec((t, N_KV_HEADS * HEAD_DIM), lambda i: (i, 0))],
        out_shape=[jax.ShapeDtypeStruct((m // ATT_BLOCK, N_Q_HEADS, ATT_BLOCK, HEAD_DIM), BF16),
                   jax.ShapeDtypeStruct((m, N_KV_HEADS * HEAD_DIM), BF16)],
        compiler_params=_cp("parallel"), name="qk_prep",
    )(proj, proj, proj, cos, sin, g_all)


def _sink_col(sink_ref, kvh):
    blk = lax.broadcasted_iota(jnp.int32, (GROUP * ATT_BLOCK, 1), 0) >> (ATT_BLOCK.bit_length() - 1)
    col = jnp.full(blk.shape, sink_ref[kvh * GROUP + GROUP - 1], F32)
    for g in range(GROUP - 2, -1, -1):
        col = jnp.where(blk == g, sink_ref[kvh * GROUP + g], col)
    return col


def _softmax_pv(scores, values, sink_col):
    m = sink_col
    for s in scores:
        m = jnp.maximum(m, jnp.max(s, axis=-1, keepdims=True))
    den = jnp.exp(sink_col - m)
    out = None
    for s, v in zip(scores, values):
        e = jnp.exp(s - m)
        den = den + jnp.sum(e, axis=-1, keepdims=True)
        pv = _dot(e.astype(BF16), v)
        out = pv if out is None else out + pv
    return out * (1.0 / den)


def _attn_kernel(sink_ref, q_ref, kp_ref, kc_ref, kn_ref, vp_ref, vc_ref, vn_ref, kx_ref, vx_ref,
                 o_ref, kwin, vwin, *, n_blocks):
    i = pl.program_id(0)
    tb = q_ref.shape[0]
    tq = tb * ATT_BLOCK
    for win, p, c, n in ((kwin, kp_ref, kc_ref, kn_ref), (vwin, vp_ref, vc_ref, vn_ref)):
        win[0:ATT_BLOCK] = p[...]
        win[ATT_BLOCK:ATT_BLOCK + tq] = c[...]
        win[ATT_BLOCK + tq:] = n[...]
    scale = HEAD_DIM ** -0.5
    shape = (GROUP * ATT_BLOCK, 3 * ATT_BLOCK)
    qpos = lax.broadcasted_iota(jnp.int32, shape, 0) & (ATT_BLOCK - 1)
    kpos = lax.broadcasted_iota(jnp.int32, shape, 1)
    band = (kpos >= qpos) & (kpos <= qpos + 2 * ATT_BLOCK)
    for b in range(tb):
        blk = i * tb + b
        lo = jnp.where(blk == 0, ATT_BLOCK, 0)
        hi = jnp.where(blk == n_blocks - 1, 2 * ATT_BLOCK, 3 * ATT_BLOCK)
        valid = band & (kpos >= lo) & (kpos < hi)
        r0 = b * ATT_BLOCK
        for kvh in range(N_KV_HEADS):
            c0 = kvh * HEAD_DIM
            q4 = q_ref[b, kvh * GROUP:(kvh + 1) * GROUP].reshape(GROUP * ATT_BLOCK, HEAD_DIM)
            s_loc = _dot_nt(q4, kwin[r0:r0 + 3 * ATT_BLOCK, c0:c0 + HEAD_DIM]) * scale
            s_loc = jnp.where(valid, s_loc, NEG_INF)
            s_ctx = _dot_nt(q4, kx_ref[:, c0:c0 + HEAD_DIM]) * scale
            o = _softmax_pv((s_loc, s_ctx),
                            (vwin[r0:r0 + 3 * ATT_BLOCK, c0:c0 + HEAD_DIM], vx_ref[:, c0:c0 + HEAD_DIM]),
                            _sink_col(sink_ref, kvh))
            for g in range(GROUP):
                h0 = (kvh * GROUP + g) * HEAD_DIM
                o_ref[r0:r0 + ATT_BLOCK, h0:h0 + HEAD_DIM] = (
                    o[g * ATT_BLOCK:(g + 1) * ATT_BLOCK].astype(o_ref.dtype))


def _attention(qn, kn, proj, kn_ctx, proj_ctx, sink):
    nb = qn.shape[0]
    m = nb * ATT_BLOCK
    cx = kn_ctx.shape[0]
    tb = min(4, nb)
    tq = tb * ATT_BLOCK
    kvw = N_KV_HEADS * HEAD_DIM
    prev = lambda i: jnp.maximum(i * tb - 1, 0)
    nxt = lambda i: jnp.minimum((i + 1) * tb, nb - 1)
    return pl.pallas_call(
        functools.partial(_attn_kernel, n_blocks=nb), grid=(nb // tb,),
        in_specs=[pl.BlockSpec(memory_space=pltpu.SMEM),
                  pl.BlockSpec((tb, N_Q_HEADS, ATT_BLOCK, HEAD_DIM), lambda i: (i, 0, 0, 0)),
                  pl.BlockSpec((ATT_BLOCK, kvw), lambda i: (prev(i), 0)),
                  pl.BlockSpec((tq, kvw), lambda i: (i, 0)),
                  pl.BlockSpec((ATT_BLOCK, kvw), lambda i: (nxt(i), 0)),
                  pl.BlockSpec((None, ATT_BLOCK, kvw), lambda i: (SLAB_KV, prev(i), 1)),
                  pl.BlockSpec((None, tq, kvw), lambda i: (SLAB_KV, i, 1)),
                  pl.BlockSpec((None, ATT_BLOCK, kvw), lambda i: (SLAB_KV, nxt(i), 1)),
                  pl.BlockSpec((cx, kvw), lambda i: (0, 0)),
                  pl.BlockSpec((None, cx, kvw), lambda i: (SLAB_KV, 0, 1))],
        out_specs=pl.BlockSpec((tq, N_Q_HEADS * HEAD_DIM), lambda i: (i, 0)),
        out_shape=jax.ShapeDtypeStruct((m, N_Q_HEADS * HEAD_DIM), BF16),
        scratch_shapes=[pltpu.VMEM((tq + 2 * ATT_BLOCK, kvw), BF16),
                        pltpu.VMEM((tq + 2 * ATT_BLOCK, kvw), BF16)],
        compiler_params=_cp("parallel"), name="attention",
    )(sink, qn, kn, kn, kn, proj, proj, proj, kn_ctx, proj_ctx)


def _ctx_attn_kernel(sink_ref, q_ref, k_ref, v_ref, o_ref):
    scale = HEAD_DIM ** -0.5
    for b in range(q_ref.shape[0]):
        r0 = b * ATT_BLOCK
        for kvh in range(N_KV_HEADS):
            c0 = kvh * HEAD_DIM
            q4 = q_ref[b, kvh * GROUP:(kvh + 1) * GROUP].reshape(GROUP * ATT_BLOCK, HEAD_DIM)
            s = _dot_nt(q4, k_ref[:, c0:c0 + HEAD_DIM]) * scale
            o = _softmax_pv((s,), (v_ref[:, c0:c0 + HEAD_DIM],), _sink_col(sink_ref, kvh))
            for g in range(GROUP):
                h0 = (kvh * GROUP + g) * HEAD_DIM
                o_ref[r0:r0 + ATT_BLOCK, h0:h0 + HEAD_DIM] = (
                    o[g * ATT_BLOCK:(g + 1) * ATT_BLOCK].astype(o_ref.dtype))


def _ctx_attention(qn, kn, proj, sink):
    nb = qn.shape[0]
    cx = nb * ATT_BLOCK
    kvw = N_KV_HEADS * HEAD_DIM
    return pl.pallas_call(
        _ctx_attn_kernel, grid=(1,),
        in_specs=[pl.BlockSpec(memory_space=pltpu.SMEM),
                  pl.BlockSpec((nb, N_Q_HEADS, ATT_BLOCK, HEAD_DIM), lambda i: (0, 0, 0, 0)),
                  pl.BlockSpec((cx, kvw), lambda i: (0, 0)),
                  pl.BlockSpec((None, cx, kvw), lambda i: (SLAB_KV, 0, 1))],
        out_specs=pl.BlockSpec((cx, N_Q_HEADS * HEAD_DIM), lambda i: (0, 0)),
        out_shape=jax.ShapeDtypeStruct((cx, N_Q_HEADS * HEAD_DIM), BF16),
        compiler_params=_cp("arbitrary"), name="ctx_attention",
    )(sink, qn, kn, proj)


def _softplus(z):
    return jnp.maximum(z, 0.0) + jnp.log1p(jnp.exp(-jnp.abs(z)))


def _gelu_tanh(x):
    return 0.5 * x * (1.0 + jnp.tanh(math.sqrt(2.0 / math.pi) * (x + 0.044715 * (x * x * x))))


def _lru_kernel(*refs, reverse, n_t, fuse):
    (u_ref, up_ref, un_ref, cw_ref, cb_ref, lam_ref, wa_ref, ba_ref, wx_ref, bx_ref, h0_ref) = refs[:11]
    if fuse:
        hf_ref, g_ref, o_ref, hfin_ref, ext, a_s, b_s, o_s, carry = refs[11:]
    else:
        o_ref, hfin_ref, ext, a_s, b_s, carry = refs[11:]
        o_s = o_ref
    i = pl.program_id(1)
    tblk = (n_t - 1 - i) if reverse else i
    t = u_ref.shape[0]
    halo = 8
    up = up_ref[...].astype(F32)[up_ref.shape[0] - halo:]
    un = un_ref[...].astype(F32)[:halo]
    ext[0:halo] = jnp.where(tblk == 0, 0.0, up)
    ext[halo:halo + t] = u_ref[...].astype(F32)
    ext[halo + t:] = jnp.where(tblk == n_t - 1, 0.0, un)
    e = ext[...]
    rows = e.shape[0]
    x = cb_ref[...] + cw_ref[1:2, :] * e[halo:halo + t]
    for k in (0, 2, 3):
        x = x + cw_ref[k:k + 1, :] * pltpu.roll(e, (1 - k) % rows, 0)[halo:halo + t]
    xb = x.astype(BF16)
    w = HEAD_DIM
    r_pre = jnp.concatenate([_dot(xb[:, n * w:(n + 1) * w], wa_ref[n]) for n in range(wa_ref.shape[0])], axis=-1)
    i_pre = jnp.concatenate([_dot(xb[:, n * w:(n + 1) * w], wx_ref[n]) for n in range(wx_ref.shape[0])], axis=-1)
    r = _sigmoid(r_pre + ba_ref[...])
    ig = _sigmoid(i_pre + bx_ref[...])
    log_a = (-LRU_C * _softplus(-lam_ref[...])) * r
    a = jnp.exp(log_a)
    a_s[...] = a
    b_s[...] = jnp.sqrt(-jnp.tanh(log_a) * (a * a + 1.0)) * (ig * x)

    @pl.when(i == 0)
    def _():
        carry[...] = h0_ref[...]

    sub = 8
    srow = lax.broadcasted_iota(jnp.int32, (sub, a_s.shape[1]), 0)

    def group(s, h):
        r0 = pl.multiple_of((t - sub - s * sub) if reverse else s * sub, sub)
        a_g = a_s[pl.ds(r0, sub), :]
        b_g = b_s[pl.ds(r0, sub), :]
        for dist in (1, 2, 4):
            if reverse:
                shift, ok = sub - dist, srow < sub - dist
            else:
                shift, ok = dist, srow >= dist
            b_g = jnp.where(ok, a_g * pltpu.roll(b_g, shift, 0) + b_g, b_g)
            a_g = jnp.where(ok, a_g * pltpu.roll(a_g, shift, 0), a_g)
        h_g = a_g * h + b_g
        o_s[pl.ds(r0, sub), :] = h_g
        return h_g[0:1] if reverse else h_g[sub - 1:sub]

    h = lax.fori_loop(0, t // sub, group, carry[0:1, :], unroll=4)
    carry[...] = jnp.broadcast_to(h, carry.shape)
    hfin_ref[...] = jnp.broadcast_to(h, hfin_ref.shape)
    if fuse:
        o_ref[...] = ((hf_ref[...] + o_s[...]) * _gelu_tanh(g_ref[...].astype(F32))).astype(o_ref.dtype)


def _lru(proj, layer, direction, h0, conv_w, conv_b, lam, wa, ba, wx, bx, h_first=None):
    m = proj.shape[1]
    width = conv_w.shape[-1]
    n_slab = width // SLAB
    t = min(1024, m)
    n_t = m // t
    hb = 16
    reverse = direction == 1
    fuse = h_first is not None
    tmap = (lambda i: n_t - 1 - i) if reverse else (lambda i: i)
    n_gate_blk = SLAB // HEAD_DIM
    vec = lambda a: a.reshape(a.shape[:-1] + (1, width))
    dvec = pl.BlockSpec((None, None, 1, SLAB), lambda s, i: (layer, direction, 0, s))
    dmat = pl.BlockSpec((None, None, n_gate_blk, HEAD_DIM, HEAD_DIM), lambda s, i: (layer, direction, s, 0, 0))
    in_specs = [
        pl.BlockSpec((None, t, SLAB), lambda s, i: (SLAB_U + s, tmap(i), 0)),
        pl.BlockSpec((None, hb, SLAB), lambda s, i: (SLAB_U + s, jnp.maximum(tmap(i) * (t // hb) - 1, 0), 0)),
        pl.BlockSpec((None, hb, SLAB), lambda s, i: (SLAB_U + s, jnp.minimum((tmap(i) + 1) * (t // hb), m // hb - 1), 0)),
        pl.BlockSpec((None, LRU_CONV, SLAB), lambda s, i: (layer, 0, s)),
        pl.BlockSpec((None, 1, SLAB), lambda s, i: (layer, 0, s)),
        dvec, dmat, dvec, dmat, dvec,
        pl.BlockSpec((8, SLAB), lambda s, i: (0, s)),
    ]
    args = [proj, proj, proj, conv_w, vec(conv_b), vec(lam), wa, vec(ba), wx, vec(bx), h0]
    seq_spec = pl.BlockSpec((t, SLAB), lambda s, i: (tmap(i), s))
    scratch = [pltpu.VMEM((t + 16, SLAB), F32), pltpu.VMEM((t, SLAB), F32), pltpu.VMEM((t, SLAB), F32)]
    if fuse:
        in_specs += [seq_spec, pl.BlockSpec((None, t, SLAB), lambda s, i: (SLAB_G + s, tmap(i), 0))]
        args += [h_first, proj]
        scratch.append(pltpu.VMEM((t, SLAB), F32))
    scratch.append(pltpu.VMEM((8, SLAB), F32))
    return pl.pallas_call(
        functools.partial(_lru_kernel, reverse=reverse, n_t=n_t, fuse=fuse),
        grid=(n_slab, n_t), in_specs=in_specs,
        out_specs=[seq_spec, pl.BlockSpec((8, SLAB), lambda s, i: (0, s))],
        out_shape=[jax.ShapeDtypeStruct((m, width), BF16 if fuse else F32),
                   jax.ShapeDtypeStruct((8, width), F32)],
        scratch_shapes=scratch,
        compiler_params=_cp("arbitrary", "arbitrary"), name="lru",
    )(*args)


def _dft_tables(n):
    ang = 2.0 * np.pi * np.outer(np.arange(n), np.arange(n)) / n
    return np.cos(ang), np.sin(ang)


def _fourier1_kernel(x_ref, f_ref, twc_ref, tws_ref, br_ref, bi_ref, xs, brs, bis):
    n1, nl2, w = x_ref.shape
    n_ct = w // LANES
    f = f_ref[...]
    x = x_ref[...].astype(F32).reshape(n1 * nl2, w)
    for c in range(n_ct):
        xs[c] = x[:, c * LANES:(c + 1) * LANES]
    for j in range(nl2):
        rows = pl.ds(j, n1, stride=nl2)
        xj = jnp.concatenate([xs[c, rows, :] for c in range(n_ct)], axis=-1)
        a = _dot(f, xj.astype(BF16))
        ar, ai = a[:n1], a[n1:]
        tc, ts = twc_ref[j], tws_ref[j]
        br = ar * tc + ai * ts
        bi = ai * tc - ar * ts
        for c in range(n_ct):
            brs[c, rows, :] = br[:, c * LANES:(c + 1) * LANES]
            bis[c, rows, :] = bi[:, c * LANES:(c + 1) * LANES]
    for c in range(n_ct):
        cs = slice(c * LANES, (c + 1) * LANES)
        br_ref[:, :, cs] = brs[c].reshape(n1, nl2, LANES).astype(br_ref.dtype)
        bi_ref[:, :, cs] = bis[c].reshape(n1, nl2, LANES).astype(bi_ref.dtype)


def _fourier2_kernel(br_ref, bi_ref, fr_ref, fi_ref, fc_ref, o_ref, os, *, scale):
    n_slab, kb, n2 = br_ref.shape[0], br_ref.shape[1], br_ref.shape[2]
    fr, fi, fc = fr_ref[...], fi_ref[...], fc_ref[...]
    gw = FOURIER_GROUP_W
    for j in range(kb):
        for sl in range(n_slab):
            b = jnp.concatenate([br_ref[sl, j], bi_ref[sl, j]], axis=0)
            pr = _dot(fr, b).astype(BF16)
            pi = _dot(fi, b).astype(BF16)
            for g in range(SLAB // gw):
                y = _dot(jnp.concatenate([pr[:, g * gw:(g + 1) * gw], pi[:, g * gw:(g + 1) * gw]], axis=-1), fc)
                y = y * scale
                for t in range(gw // LANES):
                    ct = (sl * SLAB + g * gw) // LANES + t
                    os[ct, pl.ds(j, n2, stride=kb), :] = y[:, t * LANES:(t + 1) * LANES]
    for ct in range(os.shape[0]):
        o_ref[:, :, ct * LANES:(ct + 1) * LANES] = os[ct].reshape(n2, kb, LANES).astype(o_ref.dtype)


def _fourier(proj):
    n_seg, m, _ = proj.shape
    n1 = 1 << ((m.bit_length() - 1) // 2)
    n2 = m // n1
    assert n1 * n2 == m and n1 % 16 == 0 and n2 % 16 == 0
    n_slab = 2
    width = n_slab * SLAB
    c1, s1 = _dft_tables(n1)
    f1 = jnp.asarray(np.concatenate([c1, -s1], axis=0), BF16)
    tau = 2.0 * np.pi * np.outer(np.arange(n2), np.arange(n1)) / m
    twc = jnp.asarray(np.cos(tau)[:, :, None], F32)
    tws = jnp.asarray(np.sin(tau)[:, :, None], F32)
    nl2 = 16
    b_spec = pl.BlockSpec((None, n1, nl2, SLAB), lambda s, j: (s, 0, j, 0))
    b_shape = jax.ShapeDtypeStruct((n_slab, n1, n2, SLAB), BF16)
    tw_spec = pl.BlockSpec((nl2, n1, 1), lambda s, j: (j, 0, 0))
    br, bi = pl.pallas_call(
        _fourier1_kernel, grid=(n_slab, n2 // nl2),
        in_specs=[pl.BlockSpec((None, n1, nl2, SLAB), lambda s, j: (SLAB_F + s, 0, j, 0)),
                  pl.BlockSpec((2 * n1, n1), lambda s, j: (0, 0)), tw_spec, tw_spec],
        out_specs=[b_spec, b_spec], out_shape=[b_shape, b_shape],
        scratch_shapes=[pltpu.VMEM((SLAB // LANES, n1 * nl2, LANES), F32)] * 3,
        compiler_params=_cp("parallel", "arbitrary"), name="fourier1",
    )(proj.reshape(n_seg, n1, n2, SLAB), f1, twc, tws)

    c2, s2 = _dft_tables(n2)
    cc, sc = _dft_tables(FOURIER_GROUP_W)
    kb = 16
    in4 = pl.BlockSpec((n_slab, kb, n2, SLAB), lambda k: (0, k, 0, 0))
    full = lambda a: pl.BlockSpec(a.shape, lambda k: (0, 0))
    fr = jnp.asarray(np.concatenate([c2, s2], axis=1), BF16)
    fi = jnp.asarray(np.concatenate([-s2, c2], axis=1), BF16)
    fc = jnp.asarray(np.concatenate([cc, sc], axis=0), BF16)
    out = pl.pallas_call(
        functools.partial(_fourier2_kernel, scale=1.0 / math.sqrt(m * FOURIER_GROUP_W)),
        grid=(n1 // kb,),
        in_specs=[in4, in4, full(fr), full(fi), full(fc)],
        out_specs=pl.BlockSpec((n2, kb, width), lambda k: (0, k, 0)),
        out_shape=jax.ShapeDtypeStruct((n2, n1, width), BF16),
        scratch_shapes=[pltpu.VMEM((width // LANES, n2 * kb, LANES), F32)],
        compiler_params=_cp("parallel"), name="fourier2",
    )(br, bi, fr, fi, fc)
    return out.reshape(m, width)


def _fourier_direct_kernel(u0_ref, u1_ref, cl_ref, sl_ref, cc_ref, sc_ref, o_ref, *, scale):
    gw = FOURIER_GROUP_W
    for sl, u_ref in enumerate((u0_ref, u1_ref)):
        u = u_ref[...]
        pr = _dot(cl_ref[...], u).astype(BF16)
        pi = (-_dot(sl_ref[...], u)).astype(BF16)
        for g in range(SLAB // gw):
            y = _dot(pr[:, g * gw:(g + 1) * gw], cc_ref[...]) + _dot(pi[:, g * gw:(g + 1) * gw], sc_ref[...])
            c0 = sl * SLAB + g * gw
            o_ref[:, c0:c0 + gw] = (y * scale).astype(o_ref.dtype)


def _fourier_direct(proj):
    m = proj.shape[1]
    cl, sl = _dft_tables(m)
    cc, sc = _dft_tables(FOURIER_GROUP_W)
    full = lambda n: pl.BlockSpec((n, n), lambda i: (0, 0))
    return pl.pallas_call(
        functools.partial(_fourier_direct_kernel, scale=1.0 / math.sqrt(m * FOURIER_GROUP_W)),
        grid=(1,),
        in_specs=[pl.BlockSpec((None, m, SLAB), lambda i: (SLAB_F, 0, 0)),
                  pl.BlockSpec((None, m, SLAB), lambda i: (SLAB_F + 1, 0, 0)),
                  full(m), full(m), full(FOURIER_GROUP_W), full(FOURIER_GROUP_W)],
        out_specs=pl.BlockSpec((m, 2 * SLAB), lambda i: (0, 0)),
        out_shape=jax.ShapeDtypeStruct((m, 2 * SLAB), BF16),
        compiler_params=_cp("arbitrary"), name="fourier_direct",
    )(proj, proj, jnp.asarray(cl, BF16), jnp.asarray(sl, BF16), jnp.asarray(cc, BF16), jnp.asarray(sc, BF16))


CONF_HALO = 16
CONF_ROWS = 32


def _conformer_kernel(*refs, n_t):
    cur = refs[0:4]
    prv = refs[4:8]
    nxt = refs[8:12]
    cw_ref, cb_ref, lg_ref, lb_ref, o_ref, ext, shifted, conv = refs[12:]
    i = pl.program_id(0)
    t = cur[0].shape[0]
    width = 2 * SLAB

    def glu(parts, keep):
        for sl in range(2):
            a = parts[sl][...].astype(F32)
            g = parts[2 + sl][...].astype(F32)
            yield sl, jnp.where(keep, a * _sigmoid(g), 0.0)

    for sl, v in glu(prv, i > 0):
        ext[0:CONF_HALO, sl * SLAB:(sl + 1) * SLAB] = v
    for sl, v in glu(cur, True):
        ext[CONF_HALO:CONF_HALO + t, sl * SLAB:(sl + 1) * SLAB] = v
    for sl, v in glu(nxt, i < n_t - 1):
        ext[CONF_HALO + t:, sl * SLAB:(sl + 1) * SLAB] = v
    rows = shifted.shape[1]
    for j in range(8):
        shifted[j] = ext[j:j + rows]

    n_ct = width // LANES

    def chunk(rc, _):
        r0 = pl.multiple_of(rc * CONF_ROWS, CONF_ROWS)
        for c in range(n_ct):
            cs = slice(c * LANES, (c + 1) * LANES)
            acc = [jnp.broadcast_to(cb_ref[:, cs], (8, LANES))] * (CONF_ROWS // 8)
            for k in range(CONF_KERNEL):
                off = k + 1
                wk = cw_ref[k, :, cs]
                for r in range(CONF_ROWS // 8):
                    acc[r] = acc[r] + wk * shifted[off % 8, pl.ds(r0 + (off - off % 8) + 8 * r, 8), cs]
            for r in range(CONF_ROWS // 8):
                conv[pl.ds(r0 + 8 * r, 8), cs] = acc[r]
        return 0

    lax.fori_loop(0, t // CONF_ROWS, chunk, 0)
    v = conv[...]
    d = v - jnp.mean(v, axis=-1, keepdims=True)
    y = d * lax.rsqrt(jnp.mean(d * d, axis=-1, keepdims=True) + EPS) * lg_ref[...] + lb_ref[...]
    o_ref[...] = (y * _sigmoid(y)).astype(o_ref.dtype)


def _conformer(proj, layer, conv_w_rep, conv_b, ln_g, ln_b):
    m = proj.shape[1]
    width = conv_w_rep.shape[-1]
    t = min(256, m)
    n_t = m // t
    hb = CONF_HALO
    cur = lambda s: pl.BlockSpec((None, t, SLAB), lambda i: (s, i, 0))
    prv = lambda s: pl.BlockSpec((None, hb, SLAB), lambda i: (s, jnp.maximum(i * (t // hb) - 1, 0), 0))
    nxt = lambda s: pl.BlockSpec((None, hb, SLAB), lambda i: (s, jnp.minimum((i + 1) * (t // hb), m // hb - 1), 0))
    slabs = (SLAB_CA, SLAB_CA + 1, SLAB_CG, SLAB_CG + 1)
    vec = lambda a: a.reshape(a.shape[0], 1, width)
    vspec = pl.BlockSpec((None, 1, width), lambda i: (layer, 0, 0))
    return pl.pallas_call(
        functools.partial(_conformer_kernel, n_t=n_t), grid=(n_t,),
        in_specs=[cur(s) for s in slabs] + [prv(s) for s in slabs] + [nxt(s) for s in slabs]
        + [pl.BlockSpec((None, CONF_KERNEL, 8, width), lambda i: (layer, 0, 0, 0)), vspec, vspec, vspec],
        out_specs=pl.BlockSpec((t, width), lambda i: (i, 0)),
        out_shape=jax.ShapeDtypeStruct((m, width), BF16),
        scratch_shapes=[pltpu.VMEM((t + 2 * hb, width), F32), pltpu.VMEM((8, t + 2 * hb - 8, width), F32),
                        pltpu.VMEM((t, width), F32)],
        compiler_params=_cp("parallel"), name="conformer",
    )(*([proj] * 12), conv_w_rep, vec(conv_b), vec(ln_g), vec(ln_b))


def _merge_kernel(h_ref, gw_ref, gb_ref, ya_ref, yb_ref, yc_ref, yd_ref, wb_ref, o_ref):
    gw = GATE_BLOCK_W
    acc = None
    for b, y_ref in enumerate((ya_ref, yb_ref, yc_ref, yd_ref)):
        pre = jnp.concatenate([_dot(h_ref[:, n * gw:(n + 1) * gw], gw_ref[b, n])
                               for n in range(gw_ref.shape[1])], axis=-1)
        term = _sigmoid(pre + gb_ref[b:b + 1, :]) * _dot(y_ref[...], wb_ref[b])
        acc = term if acc is None else acc + term
    o_ref[...] = acc.astype(o_ref.dtype)


def _merge(h, ys, gate_w, gate_b, w_branch, layer):
    m, d = h.shape
    bw = ys[0].shape[-1]
    nbr = len(ys)
    tm = min(1024, m)
    gpb = 2
    tn = gpb * GATE_BLOCK_W
    y_spec = pl.BlockSpec((tm, bw), lambda i, n: (i, 0))
    return pl.pallas_call(
        _merge_kernel, grid=(m // tm, d // tn),
        in_specs=[pl.BlockSpec((tm, tn), lambda i, n: (i, n)),
                  pl.BlockSpec((None, nbr, gpb, GATE_BLOCK_W, GATE_BLOCK_W), lambda i, n: (layer, 0, n, 0, 0)),
                  pl.BlockSpec((None, nbr, tn), lambda i, n: (layer, 0, n)),
                  y_spec, y_spec, y_spec, y_spec,
                  pl.BlockSpec((None, nbr, bw, tn), lambda i, n: (layer, 0, 0, n))],
        out_specs=pl.BlockSpec((tm, tn), lambda i, n: (i, n)),
        out_shape=jax.ShapeDtypeStruct((m, d), BF16),
        compiler_params=_cp("parallel", "arbitrary"), name="merge",
    )(h, gate_w, gate_b, *ys, w_branch)


def kernel(x, c, ctx, c_ctx, ada_w, ada_b, norm_mix_g, norm_ffn_g, w_in, qk_norm_g, attn_sink,
           lru_conv_w, lru_conv_b, lru_lambda, lru_wa, lru_ba, lru_wx, lru_bx,
           conf_conv_w, conf_conv_b, conf_ln_g, conf_ln_b, w_branch, gate_w, gate_b, w_out,
           ffn_w_gate, ffn_w_up, ffn_w_down, router_w, router_b, moe_w_gate, moe_w_up, moe_w_down):
    assert x.shape[0] == 1 and ctx.shape[0] == 1 and c.shape[0] == 1
    depth = ada_w.shape[0]
    seq, d = x.shape[1], x.shape[2]
    cx = ctx.shape[1]
    xl = x[0]
    xc = ctx[0]

    w_in_b, w_branch_b, gate_w_b, w_out_b = (a.astype(BF16) for a in (w_in, w_branch, gate_w, w_out))
    ffn_g_b, ffn_u_b, ffn_d_b = (a.astype(BF16) for a in (ffn_w_gate, ffn_w_up, ffn_w_down))
    moe_gu_b = jnp.concatenate([moe_w_gate.astype(BF16), moe_w_up.astype(BF16)], axis=-1)
    moe_d_b = moe_w_down.astype(BF16).reshape(moe_w_down.shape[0], -1, d)
    lru_wa_b, lru_wx_b = lru_wa.astype(BF16), lru_wx.astype(BF16)
    conf_w_rep = jnp.broadcast_to(conf_conv_w[:, :, None, :], conf_conv_w.shape[:2] + (8,) + conf_conv_w.shape[2:])
    router_w_p = jnp.pad(router_w, ((0, 0), (0, 0), (0, LANES - N_EXPERTS)))
    router_b_p = jnp.pad(router_b, ((0, 0), (0, LANES - N_EXPERTS)))[:, None, :]

    c_rows = jnp.concatenate([c, c_ctx[None, :], jnp.zeros((6, d), F32)], axis=0)
    mod = _adaln(c_rows, ada_w, ada_b)

    cos_l, sin_l = _rope_tables(seq)
    cos_c, sin_c = jnp.ones((cx, HEAD_DIM), F32), jnp.zeros((cx, HEAD_DIM), F32)
    zero_state = jnp.zeros((8, lru_conv_w.shape[-1]), F32)
    lru_p = (lru_conv_w, lru_conv_b, lru_lambda, lru_wa_b, lru_ba, lru_wx_b, lru_bx)

    def channel_mixer(xs, l, row):
        li = l // 2
        if l % 2 == 0:
            h = _norm(xs, norm_ffn_g, mod, l, 3, row)
            hid = _ffn_up(h, ffn_g_b, ffn_u_b, li)
            return _mm_res(hid, ffn_d_b, li, xs, mod, l, 5, row)
        h, gates = _norm(xs, norm_ffn_g, mod, l, 3, row, router=(router_w_p, router_b_p, li))
        hid = _moe_up(h, gates, moe_gu_b, li)
        return _mm_res(hid, moe_d_b, li, xs, mod, l, 5, row)

    for l in range(depth):
        need_ctx = l < depth - 1
        h_l = _norm(xl, norm_mix_g, mod, l, 0, 0)
        h_c = _norm(xc, norm_mix_g, mod, l, 0, 1)
        p_l = _in_proj(h_l, w_in_b, l)
        p_c = _in_proj(h_c, w_in_b, l)
        qn_l, kn_l = _qk_prep(p_l, cos_l, sin_l, qk_norm_g, l)
        qn_c, kn_c = _qk_prep(p_c, cos_c, sin_c, qk_norm_g, l)
        sink = attn_sink[l]
        ya_l = _attention(qn_l, kn_l, p_l, kn_c, p_c, sink)
        hf_c, fin_f = _lru(p_c, l, 0, zero_state, *lru_p)
        yb_c, fin_b = _lru(p_c, l, 1, zero_state, *lru_p, h_first=hf_c if need_ctx else None)
        hf_l, _ = _lru(p_l, l, 0, fin_f, *lru_p)
        yb_l, _ = _lru(p_l, l, 1, fin_b, *lru_p, h_first=hf_l)
        yc_l = _fourier(p_l)
        yd_l = _conformer(p_l, l, conf_w_rep, conf_conv_b, conf_ln_g, conf_ln_b)
        acc_l = _merge(h_l, (ya_l, yb_l, yc_l, yd_l), gate_w_b, gate_b, w_branch_b, l)
        xl = _mm_res(acc_l, w_out_b, l, xl, mod, l, 2, 0)
        xl = channel_mixer(xl, l, 0)
        if need_ctx:
            ya_c = _ctx_attention(qn_c, kn_c, p_c, sink)
            yc_c = _fourier_direct(p_c)
            yd_c = _conformer(p_c, l, conf_w_rep, conf_conv_b, conf_ln_g, conf_ln_b)
            acc_c = _merge(h_c, (ya_c, yb_c, yc_c, yd_c), gate_w_b, gate_b, w_branch_b, l)
            xc = _mm_res(acc_c, w_out_b, l, xc, mod, l, 2, 1)
            xc = channel_mixer(xc, l, 1)
    return xl[None]
```

```python
import functools
import math

import numpy as np
import jax
import jax.numpy as jnp
from jax import lax
from jax.experimental import pallas as pl
from jax.experimental.pallas import tpu as pltpu

F32 = jnp.float32
BF16 = jnp.bfloat16

HEAD_DIM = 128
N_Q_HEADS = 8
N_KV_HEADS = 2
GROUP = N_Q_HEADS // N_KV_HEADS
ATT_BLOCK = 128
GRID_W = 64
ROPE_THETA = 10000.0
NEG_INF = -1e30
LRU_C = 8.0
LRU_CONV = 4
CONF_KERNEL = 31
FOURIER_GROUP_W = 256
GATE_BLOCK_W = 256
N_EXPERTS = 8
EPS = 1e-6
SLAB = 512
SLAB_Q, SLAB_KV, SLAB_U, SLAB_G, SLAB_F, SLAB_CA, SLAB_CG = 0, 2, 3, 5, 7, 9, 11

V7X_VMEM_LIMIT = 56 * 1024 * 1024
LANES = 128


def _cp(*sem):
    return pltpu.CompilerParams(dimension_semantics=sem, vmem_limit_bytes=V7X_VMEM_LIMIT)


def _sigmoid(x):
    return 1.0 / (1.0 + jnp.exp(-x))


def _dot(a, b):
    return jnp.dot(a, b, preferred_element_type=F32)


def _dot_nt(a, b):
    return lax.dot_general(a, b, (((1,), (1,)), ((), ())), preferred_element_type=F32)


def _split_bf16(x):
    hi = x.astype(BF16)
    lo = (x - hi.astype(F32)).astype(BF16)
    return hi, lo


def _adaln_kernel(c_ref, w_ref, b_ref, o_ref):
    c = c_ref[...]
    s_hi, s_lo = _split_bf16(c * _sigmoid(c))
    w_hi, w_lo = _split_bf16(w_ref[...])
    o_ref[...] = _dot(s_hi, w_hi) + _dot(s_lo, w_hi) + _dot(s_hi, w_lo) + b_ref[...]


def _adaln(c_rows, ada_w, ada_b):
    depth, d, n = ada_w.shape
    tn = 512
    return pl.pallas_call(
        _adaln_kernel,
        grid=(depth, n // tn),
        in_specs=[pl.BlockSpec((8, d), lambda l, j: (0, 0)),
                  pl.BlockSpec((None, d, tn), lambda l, j: (l, 0, j)),
                  pl.BlockSpec((None, 1, tn), lambda l, j: (l, 0, j))],
        out_specs=pl.BlockSpec((None, 8, tn), lambda l, j: (l, 0, j)),
        out_shape=jax.ShapeDtypeStruct((depth, 8, n), F32),
        compiler_params=_cp("arbitrary", "arbitrary"),
        name="adaln",
    )(c_rows, ada_w, ada_b.reshape(depth, 1, n))


def _norm_mod(x_ref, g_ref, sh_ref, sc_ref, row):
    x = x_ref[...]
    y = x * lax.rsqrt(jnp.mean(x * x, axis=-1, keepdims=True) + EPS) * g_ref[...]
    return y * (1.0 + sc_ref[row:row + 1, :]) + sh_ref[row:row + 1, :]


def _norm_kernel(x_ref, g_ref, sh_ref, sc_ref, o_ref, *, row):
    o_ref[...] = _norm_mod(x_ref, g_ref, sh_ref, sc_ref, row).astype(BF16)


def _norm_router_kernel(x_ref, g_ref, sh_ref, sc_ref, rw_ref, rb_ref, o_ref, gates_ref, *, row):
    h = _norm_mod(x_ref, g_ref, sh_ref, sc_ref, row)
    o_ref[...] = h.astype(BF16)
    h_hi, h_lo = _split_bf16(h)
    w_hi, w_lo = _split_bf16(rw_ref[...])
    logits = _dot(h_hi, w_hi) + _dot(h_lo, w_hi) + _dot(h_hi, w_lo) + rb_ref[...]
    lane = lax.broadcasted_iota(jnp.int32, logits.shape, 1)
    logits = jnp.where(lane < N_EXPERTS, logits, -jnp.inf)
    m1 = jnp.max(logits, axis=-1, keepdims=True)
    i1 = jnp.min(jnp.where(logits == m1, lane, LANES), axis=-1, keepdims=True)
    rest = jnp.where(lane == i1, -jnp.inf, logits)
    m2 = jnp.max(rest, axis=-1, keepdims=True)
    i2 = jnp.min(jnp.where(rest == m2, lane, LANES), axis=-1, keepdims=True)
    e2 = jnp.exp(m2 - m1)
    inv = 1.0 / (1.0 + e2)
    gates_ref[...] = jnp.where(lane == i1, inv, 0.0) + jnp.where(lane == i2, e2 * inv, 0.0)


def _norm(x, g_all, mod, layer, chunk, row, router=None):
    m, d = x.shape
    tm = min(512, m)
    in_specs = [pl.BlockSpec((tm, d), lambda i: (i, 0)),
                pl.BlockSpec((None, 1, d), lambda i: (layer, 0, 0)),
                pl.BlockSpec((None, 8, d), lambda i: (layer, 0, chunk)),
                pl.BlockSpec((None, 8, d), lambda i: (layer, 0, chunk + 1))]
    args = [x, g_all.reshape(g_all.shape[0], 1, d), mod, mod]
    h_spec = pl.BlockSpec((tm, d), lambda i: (i, 0))
    h_shape = jax.ShapeDtypeStruct((m, d), BF16)
    if router is None:
        return pl.pallas_call(
            functools.partial(_norm_kernel, row=row), grid=(m // tm,), in_specs=in_specs,
            out_specs=h_spec, out_shape=h_shape, compiler_params=_cp("parallel"), name="norm",
        )(*args)
    rw, rb, li = router
    in_specs += [pl.BlockSpec((None, d, LANES), lambda i: (li, 0, 0)),
                 pl.BlockSpec((None, 1, LANES), lambda i: (li, 0, 0))]
    return pl.pallas_call(
        functools.partial(_norm_router_kernel, row=row), grid=(m // tm,), in_specs=in_specs,
        out_specs=[h_spec, pl.BlockSpec((tm, LANES), lambda i: (i, 0))],
        out_shape=[h_shape, jax.ShapeDtypeStruct((m, LANES), F32)],
        compiler_params=_cp("parallel"), name="norm_router",
    )(*args, rw, rb)


def _mm_kernel(a_ref, w_ref, o_ref):
    o_ref[...] = _dot(a_ref[...], w_ref[...]).astype(o_ref.dtype)


def _in_proj(h, w_all, layer):
    m, d = h.shape
    n = w_all.shape[-1]
    tm = min(2048, m)
    return pl.pallas_call(
        _mm_kernel,
        grid=(m // tm, n // SLAB),
        in_specs=[pl.BlockSpec((tm, d), lambda i, j: (i, 0)),
                  pl.BlockSpec((None, d, SLAB), lambda i, j: (layer, 0, j))],
        out_specs=pl.BlockSpec((None, tm, SLAB), lambda i, j: (j, i, 0)),
        out_shape=jax.ShapeDtypeStruct((n // SLAB, m, SLAB), BF16),
        compiler_params=_cp("parallel", "arbitrary"), name="in_proj",
    )(h, w_all)


def _mm_res_kernel(a_ref, w_ref, x_ref, gate_ref, o_ref, *, row):
    y = _dot(a_ref[...], w_ref[...])
    o_ref[...] = x_ref[...] + gate_ref[row:row + 1, :] * y


def _mm_res(a, w_all, layer, x, mod, mod_layer, chunk, row):
    m, k = a.shape
    d = x.shape[-1]
    tm = min(1024, m)
    tn = 512
    gate_off = chunk * d // tn
    return pl.pallas_call(
        functools.partial(_mm_res_kernel, row=row),
        grid=(m // tm, d // tn),
        in_specs=[pl.BlockSpec((tm, k), lambda i, j: (i, 0)),
                  pl.BlockSpec((None, k, tn), lambda i, j: (layer, 0, j)),
                  pl.BlockSpec((tm, tn), lambda i, j: (i, j)),
                  pl.BlockSpec((None, 8, tn), lambda i, j: (mod_layer, 0, gate_off + j))],
        out_specs=pl.BlockSpec((tm, tn), lambda i, j: (i, j)),
        out_shape=jax.ShapeDtypeStruct((m, d), F32),
        compiler_params=_cp("parallel", "arbitrary"), name="mm_res",
    )(a, w_all, x, mod)


def _ffn_up_kernel(h_ref, wg_ref, wu_ref, o_ref):
    h = h_ref[...]
    g = _dot(h, wg_ref[...])
    o_ref[...] = ((g * _sigmoid(g)) * _dot(h, wu_ref[...])).astype(o_ref.dtype)


def _moe_up_kernel(h_ref, wgu_ref, gates_ref, o_ref):
    fe = o_ref.shape[-1]
    gu = _dot(h_ref[...], wgu_ref[...])
    g = gu[:, :fe]
    hid = (g * _sigmoid(g)) * gu[:, fe:]
    gates = gates_ref[...]
    lane = lax.broadcasted_iota(jnp.int32, gates.shape, 1)
    gate = jnp.sum(jnp.where(lane == pl.program_id(1), gates, 0.0), axis=-1, keepdims=True)
    o_ref[...] = (hid * gate).astype(o_ref.dtype)


def _ffn_up(h, wg_all, wu_all, li):
    m, d = h.shape
    f = wg_all.shape[-1]
    tm = min(1024, m)
    tn = 512
    w_spec = pl.BlockSpec((None, d, tn), lambda i, j: (li, 0, j))
    return pl.pallas_call(
        _ffn_up_kernel, grid=(m // tm, f // tn),
        in_specs=[pl.BlockSpec((tm, d), lambda i, j: (i, 0)), w_spec, w_spec],
        out_specs=pl.BlockSpec((tm, tn), lambda i, j: (i, j)),
        out_shape=jax.ShapeDtypeStruct((m, f), BF16),
        compiler_params=_cp("parallel", "arbitrary"), name="ffn_up",
    )(h, wg_all, wu_all)


def _moe_up(h, gates, wgu_all, li):
    m, d = h.shape
    n_e, fe = wgu_all.shape[1], wgu_all.shape[-1] // 2
    tm = min(1024, m)
    return pl.pallas_call(
        _moe_up_kernel, grid=(m // tm, n_e),
        in_specs=[pl.BlockSpec((tm, d), lambda i, e: (i, 0)),
                  pl.BlockSpec((None, None, d, 2 * fe), lambda i, e: (li, e, 0, 0)),
                  pl.BlockSpec((tm, LANES), lambda i, e: (i, 0))],
        out_specs=pl.BlockSpec((tm, fe), lambda i, e: (i, e)),
        out_shape=jax.ShapeDtypeStruct((m, n_e * fe), BF16),
        compiler_params=_cp("parallel", "arbitrary"), name="moe_up",
    )(h, wgu_all, gates)


def _rope_tables(length):
    rows = length // GRID_W
    row = jnp.broadcast_to(jnp.arange(rows, dtype=F32)[:, None], (rows, GRID_W)).reshape(-1)
    col = jnp.broadcast_to(jnp.arange(GRID_W, dtype=F32)[None, :], (rows, GRID_W)).reshape(-1)
    n_freq = HEAD_DIM // 4
    inv_freq = ROPE_THETA ** (-jnp.arange(n_freq, dtype=F32) / n_freq)
    ang_r = row[:, None] * inv_freq
    ang_c = col[:, None] * inv_freq
    cr, sr, cc, sc = jnp.cos(ang_r), jnp.sin(ang_r), jnp.cos(ang_c), jnp.sin(ang_c)
    return (jnp.concatenate([cr, cr, cc, cc], axis=-1), jnp.concatenate([-sr, sr, -sc, sc], axis=-1))


def _qk_prep_kernel(q0_ref, q1_ref, kv_ref, cos_ref, sin_ref, g_ref, qn_ref, kn_ref):
    cos = cos_ref[...]
    sin = sin_ref[...]
    t = cos.shape[0]
    lane = lax.broadcasted_iota(jnp.int32, (t, HEAD_DIM), 1)
    low_quarter = (lane & 63) < 32

    def prep(xh, g):
        xh = xh.astype(F32)
        y = xh * lax.rsqrt(jnp.mean(xh * xh, axis=-1, keepdims=True) + EPS) * g
        partner = jnp.where(low_quarter, pltpu.roll(y, HEAD_DIM - 32, 1), pltpu.roll(y, 32, 1))
        return (y * cos + partner * sin).astype(BF16)

    qg = g_ref[0:1, :]
    kg = g_ref[1:2, :]
    for h in range(N_Q_HEADS):
        src = q0_ref if h < GROUP else q1_ref
        c0 = (h % GROUP) * HEAD_DIM
        qh = prep(src[:, c0:c0 + HEAD_DIM], qg)
        for b in range(t // ATT_BLOCK):
            qn_ref[b, h] = qh[b * ATT_BLOCK:(b + 1) * ATT_BLOCK]
    for h in range(N_KV_HEADS):
        c0 = h * HEAD_DIM
        kn_ref[:, c0:c0 + HEAD_DIM] = prep(kv_ref[:, c0:c0 + HEAD_DIM], kg)


def _qk_prep(proj, cos, sin, g_all, layer):
    m = proj.shape[1]
    t = min(512, m)
    tb = t // ATT_BLOCK
    slab = lambda s: pl.BlockSpec((None, t, SLAB), lambda i: (s, i, 0))
    tab = pl.BlockSpec((t, HEAD_DIM), lambda i: (i, 0))
    return pl.pallas_call(
        _qk_prep_kernel, grid=(m // t,),
        in_specs=[slab(SLAB_Q), slab(SLAB_Q + 1),
                  pl.BlockSpec((None, t, N_KV_HEADS * HEAD_DIM), lambda i: (SLAB_KV, i, 0)),
                  tab, tab, pl.BlockSpec((None, 2, HEAD_DIM), lambda i: (layer, 0, 0))],
        out_specs=[pl.BlockSpec((tb, N_Q_HEADS, ATT_BLOCK, HEAD_DIM), lambda i: (i, 0, 0, 0)),
                   pl.BlockSpec((t, N_KV_HEADS * HEAD_DIM), lambda i: (i, 0))],
        out_shape=[jax.ShapeDtypeStruct((m // ATT_BLOCK, N_Q_HEADS, ATT_BLOCK, HEAD_DIM), BF16),
                   jax.ShapeDtypeStruct((m, N_KV_HEADS * HEAD_DIM), BF16)],
        compiler_params=_cp("parallel"), name="qk_prep",
    )(proj, proj, proj, cos, sin, g_all)


def _sink_col(sink_ref, kvh):
    blk = lax.broadcasted_iota(jnp.int32, (GROUP * ATT_BLOCK, 1), 0) >> (ATT_BLOCK.bit_length() - 1)
    col = jnp.full(blk.shape, sink_ref[kvh * GROUP + GROUP - 1], F32)
    for g in range(GROUP - 2, -1, -1):
        col = jnp.where(blk == g, sink_ref[kvh * GROUP + g], col)
    return col


def _softmax_pv(scores, values, sink_col):
    m = sink_col
    for s in scores:
        m = jnp.maximum(m, jnp.max(s, axis=-1, keepdims=True))
    den = jnp.exp(sink_col - m)
    out = None
    for s, v in zip(scores, values):
        e = jnp.exp(s - m)
        den = den + jnp.sum(e, axis=-1, keepdims=True)
        pv = _dot(e.astype(BF16), v)
        out = pv if out is None else out + pv
    return out * (1.0 / den)


def _attn_kernel(sink_ref, q_ref, kp_ref, kc_ref, kn_ref, vp_ref, vc_ref, vn_ref, kx_ref, vx_ref,
                 o_ref, kwin, vwin, *, n_blocks):
    i = pl.program_id(0)
    tb = q_ref.shape[0]
    tq = tb * ATT_BLOCK
    for win, p, c, n in ((kwin, kp_ref, kc_ref, kn_ref), (vwin, vp_ref, vc_ref, vn_ref)):
        win[0:ATT_BLOCK] = p[...]
        win[ATT_BLOCK:ATT_BLOCK + tq] = c[...]
        win[ATT_BLOCK + tq:] = n[...]
    scale = HEAD_DIM ** -0.5
    shape = (GROUP * ATT_BLOCK, 3 * ATT_BLOCK)
    qpos = lax.broadcasted_iota(jnp.int32, shape, 0) & (ATT_BLOCK - 1)
    kpos = lax.broadcasted_iota(jnp.int32, shape, 1)
    band = (kpos >= qpos) & (kpos <= qpos + 2 * ATT_BLOCK)
    for b in range(tb):
        blk = i * tb + b
        lo = jnp.where(blk == 0, ATT_BLOCK, 0)
        hi = jnp.where(blk == n_blocks - 1, 2 * ATT_BLOCK, 3 * ATT_BLOCK)
        valid = band & (kpos >= lo) & (kpos < hi)
        r0 = b * ATT_BLOCK
        for kvh in range(N_KV_HEADS):
            c0 = kvh * HEAD_DIM
            q4 = q_ref[b, kvh * GROUP:(kvh + 1) * GROUP].reshape(GROUP * ATT_BLOCK, HEAD_DIM)
            s_loc = _dot_nt(q4, kwin[r0:r0 + 3 * ATT_BLOCK, c0:c0 + HEAD_DIM]) * scale
            s_loc = jnp.where(valid, s_loc, NEG_INF)
            s_ctx = _dot_nt(q4, kx_ref[:, c0:c0 + HEAD_DIM]) * scale
            o = _softmax_pv((s_loc, s_ctx),
                            (vwin[r0:r0 + 3 * ATT_BLOCK, c0:c0 + HEAD_DIM], vx_ref[:, c0:c0 + HEAD_DIM]),
                            _sink_col(sink_ref, kvh))
            for g in range(GROUP):
                h0 = (kvh * GROUP + g) * HEAD_DIM
                o_ref[r0:r0 + ATT_BLOCK, h0:h0 + HEAD_DIM] = (
                    o[g * ATT_BLOCK:(g + 1) * ATT_BLOCK].astype(o_ref.dtype))


def _attention(qn, kn, proj, kn_ctx, proj_ctx, sink):
    nb = qn.shape[0]
    m = nb * ATT_BLOCK
    cx = kn_ctx.shape[0]
    tb = min(4, nb)
    tq = tb * ATT_BLOCK
    kvw = N_KV_HEADS * HEAD_DIM
    prev = lambda i: jnp.maximum(i * tb - 1, 0)
    nxt = lambda i: jnp.minimum((i + 1) * tb, nb - 1)
    return pl.pallas_call(
        functools.partial(_attn_kernel, n_blocks=nb), grid=(nb // tb,),
        in_specs=[pl.BlockSpec(memory_space=pltpu.SMEM),
                  pl.BlockSpec((tb, N_Q_HEADS, ATT_BLOCK, HEAD_DIM), lambda i: (i, 0, 0, 0)),
                  pl.BlockSpec((ATT_BLOCK, kvw), lambda i: (prev(i), 0)),
                  pl.BlockSpec((tq, kvw), lambda i: (i, 0)),
                  pl.BlockSpec((ATT_BLOCK, kvw), lambda i: (nxt(i), 0)),
                  pl.BlockSpec((None, ATT_BLOCK, kvw), lambda i: (SLAB_KV, prev(i), 1)),
                  pl.BlockSpec((None, tq, kvw), lambda i: (SLAB_KV, i, 1)),
                  pl.BlockSpec((None, ATT_BLOCK, kvw), lambda i: (SLAB_KV, nxt(i), 1)),
                  pl.BlockSpec((cx, kvw), lambda i: (0, 0)),
                  pl.BlockSpec((None, cx, kvw), lambda i: (SLAB_KV, 0, 1))],
        out_specs=pl.BlockSpec((tq, N_Q_HEADS * HEAD_DIM), lambda i: (i, 0)),
        out_shape=jax.ShapeDtypeStruct((m, N_Q_HEADS * HEAD_DIM), BF16),
        scratch_shapes=[pltpu.VMEM((tq + 2 * ATT_BLOCK, kvw), BF16),
                        pltpu.VMEM((tq + 2 * ATT_BLOCK, kvw), BF16)],
        compiler_params=_cp("parallel"), name="attention",
    )(sink, qn, kn, kn, kn, proj, proj, proj, kn_ctx, proj_ctx)


def _ctx_attn_kernel(sink_ref, q_ref, k_ref, v_ref, o_ref):
    scale = HEAD_DIM ** -0.5
    for b in range(q_ref.shape[0]):
        r0 = b * ATT_BLOCK
        for kvh in range(N_KV_HEADS):
            c0 = kvh * HEAD_DIM
            q4 = q_ref[b, kvh * GROUP:(kvh + 1) * GROUP].reshape(GROUP * ATT_BLOCK, HEAD_DIM)
            s = _dot_nt(q4, k_ref[:, c0:c0 + HEAD_DIM]) * scale
            o = _softmax_pv((s,), (v_ref[:, c0:c0 + HEAD_DIM],), _sink_col(sink_ref, kvh))
            for g in range(GROUP):
                h0 = (kvh * GROUP + g) * HEAD_DIM
                o_ref[r0:r0 + ATT_BLOCK, h0:h0 + HEAD_DIM] = (
                    o[g * ATT_BLOCK:(g + 1) * ATT_BLOCK].astype(o_ref.dtype))


def _ctx_attention(qn, kn, proj, sink):
    nb = qn.shape[0]
    cx = nb * ATT_BLOCK
    kvw = N_KV_HEADS * HEAD_DIM
    return pl.pallas_call(
        _ctx_attn_kernel, grid=(1,),
        in_specs=[pl.BlockSpec(memory_space=pltpu.SMEM),
                  pl.BlockSpec((nb, N_Q_HEADS, ATT_BLOCK, HEAD_DIM), lambda i: (0, 0, 0, 0)),
                  pl.BlockSpec((cx, kvw), lambda i: (0, 0)),
                  pl.BlockSpec((None, cx, kvw), lambda i: (SLAB_KV, 0, 1))],
        out_specs=pl.BlockSpec((cx, N_Q_HEADS * HEAD_DIM), lambda i: (0, 0)),
        out_shape=jax.ShapeDtypeStruct((cx, N_Q_HEADS * HEAD_DIM), BF16),
        compiler_params=_cp("arbitrary"), name="ctx_attention",
    )(sink, qn, kn, proj)


def _softplus(z):
    return jnp.maximum(z, 0.0) + jnp.log1p(jnp.exp(-jnp.abs(z)))


def _gelu_tanh(x):
    return 0.5 * x * (1.0 + jnp.tanh(math.sqrt(2.0 / math.pi) * (x + 0.044715 * (x * x * x))))


def _lru_kernel(*refs, reverse, n_t, fuse):
    (u_ref, up_ref, un_ref, cw_ref, cb_ref, lam_ref, wa_ref, ba_ref, wx_ref, bx_ref, h0_ref) = refs[:11]
    if fuse:
        hf_ref, g_ref, o_ref, hfin_ref, ext, a_s, b_s, o_s, carry = refs[11:]
    else:
        o_ref, hfin_ref, ext, a_s, b_s, carry = refs[11:]
        o_s = o_ref
    i = pl.program_id(1)
    tblk = (n_t - 1 - i) if reverse else i
    t = u_ref.shape[0]
    halo = 8
    up = up_ref[...].astype(F32)[up_ref.shape[0] - halo:]
    un = un_ref[...].astype(F32)[:halo]
    ext[0:halo] = jnp.where(tblk == 0, 0.0, up)
    ext[halo:halo + t] = u_ref[...].astype(F32)
    ext[halo + t:] = jnp.where(tblk == n_t - 1, 0.0, un)
    e = ext[...]
    rows = e.shape[0]
    x = cb_ref[...] + cw_ref[1:2, :] * e[halo:halo + t]
    for k in (0, 2, 3):
        x = x + cw_ref[k:k + 1, :] * pltpu.roll(e, (1 - k) % rows, 0)[halo:halo + t]
    xb = x.astype(BF16)
    w = HEAD_DIM
    r_pre = jnp.concatenate([_dot(xb[:, n * w:(n + 1) * w], wa_ref[n]) for n in range(wa_ref.shape[0])], axis=-1)
    i_pre = jnp.concatenate([_dot(xb[:, n * w:(n + 1) * w], wx_ref[n]) for n in range(wx_ref.shape[0])], axis=-1)
    r = _sigmoid(r_pre + ba_ref[...])
    ig = _sigmoid(i_pre + bx_ref[...])
    log_a = (-LRU_C * _softplus(-lam_ref[...])) * r
    a = jnp.exp(log_a)
    a_s[...] = a
    b_s[...] = jnp.sqrt(-jnp.tanh(log_a) * (a * a + 1.0)) * (ig * x)

    @pl.when(i == 0)
    def _():
        carry[...] = h0_ref[...]

    sub = 8
    srow = lax.broadcasted_iota(jnp.int32, (sub, a_s.shape[1]), 0)

    def group(s, h):
        r0 = pl.multiple_of((t - sub - s * sub) if reverse else s * sub, sub)
        a_g = a_s[pl.ds(r0, sub), :]
        b_g = b_s[pl.ds(r0, sub), :]
        for dist in (1, 2, 4):
            if reverse:
                shift, ok = sub - dist, srow < sub - dist
            else:
                shift, ok = dist, srow >= dist
            b_g = jnp.where(ok, a_g * pltpu.roll(b_g, shift, 0) + b_g, b_g)
            a_g = jnp.where(ok, a_g * pltpu.roll(a_g, shift, 0), a_g)
        h_g = a_g * h + b_g
        o_s[pl.ds(r0, sub), :] = h_g
        return h_g[0:1] if reverse else h_g[sub - 1:sub]

    h = lax.fori_loop(0, t // sub, group, carry[0:1, :], unroll=4)
    carry[...] = jnp.broadcast_to(h, carry.shape)
    hfin_ref[...] = jnp.broadcast_to(h, hfin_ref.shape)
    if fuse:
        o_ref[...] = ((hf_ref[...] + o_s[...]) * _gelu_tanh(g_ref[...].astype(F32))).astype(o_ref.dtype)


def _lru(proj, layer, direction, h0, conv_w, conv_b, lam, wa, ba, wx, bx, h_first=None):
    m = proj.shape[1]
    width = conv_w.shape[-1]
    n_slab = width // SLAB
    t = min(1024, m)
    n_t = m // t
    hb = 16
    reverse = direction == 1
    fuse = h_first is not None
    tmap = (lambda i: n_t - 1 - i) if reverse else (lambda i: i)
    n_gate_blk = SLAB // HEAD_DIM
    vec = lambda a: a.reshape(a.shape[:-1] + (1, width))
    dvec = pl.BlockSpec((None, None, 1, SLAB), lambda s, i: (layer, direction, 0, s))
    dmat = pl.BlockSpec((None, None, n_gate_blk, HEAD_DIM, HEAD_DIM), lambda s, i: (layer, direction, s, 0, 0))
    in_specs = [
        pl.BlockSpec((None, t, SLAB), lambda s, i: (SLAB_U + s, tmap(i), 0)),
        pl.BlockSpec((None, hb, SLAB), lambda s, i: (SLAB_U + s, jnp.maximum(tmap(i) * (t // hb) - 1, 0), 0)),
        pl.BlockSpec((None, hb, SLAB), lambda s, i: (SLAB_U + s, jnp.minimum((tmap(i) + 1) * (t // hb), m // hb - 1), 0)),
        pl.BlockSpec((None, LRU_CONV, SLAB), lambda s, i: (layer, 0, s)),
        pl.BlockSpec((None, 1, SLAB), lambda s, i: (layer, 0, s)),
        dvec, dmat, dvec, dmat, dvec,
        pl.BlockSpec((8, SLAB), lambda s, i: (0, s)),
    ]
    args = [proj, proj, proj, conv_w, vec(conv_b), vec(lam), wa, vec(ba), wx, vec(bx), h0]
    seq_spec = pl.BlockSpec((t, SLAB), lambda s, i: (tmap(i), s))
    scratch = [pltpu.VMEM((t + 16, SLAB), F32), pltpu.VMEM((t, SLAB), F32), pltpu.VMEM((t, SLAB), F32)]
    if fuse:
        in_specs += [seq_spec, pl.BlockSpec((None, t, SLAB), lambda s, i: (SLAB_G + s, tmap(i), 0))]
        args += [h_first, proj]
        scratch.append(pltpu.VMEM((t, SLAB), F32))
    scratch.append(pltpu.VMEM((8, SLAB), F32))
    return pl.pallas_call(
        functools.partial(_lru_kernel, reverse=reverse, n_t=n_t, fuse=fuse),
        grid=(n_slab, n_t), in_specs=in_specs,
        out_specs=[seq_spec, pl.BlockSpec((8, SLAB), lambda s, i: (0, s))],
        out_shape=[jax.ShapeDtypeStruct((m, width), BF16 if fuse else F32),
                   jax.ShapeDtypeStruct((8, width), F32)],
        scratch_shapes=scratch,
        compiler_params=_cp("arbitrary", "arbitrary"), name="lru",
    )(*args)


def _dft_tables(n):
    ang = 2.0 * np.pi * np.outer(np.arange(n), np.arange(n)) / n
    return np.cos(ang), np.sin(ang)


def _fourier1_kernel(x_ref, f_ref, twc_ref, tws_ref, br_ref, bi_ref, xs, brs, bis):
    n1, nl2, w = x_ref.shape
    n_ct = w // LANES
    sub = 8
    f = f_ref[...]
    x = x_ref[...].astype(F32)
    for hf in range(nl2 // sub):
        xh = x[:, hf * sub:(hf + 1) * sub, :].reshape(n1 * sub, w)
        for c in range(n_ct):
            xs[hf, c] = xh[:, c * LANES:(c + 1) * LANES]
    for j in range(nl2):
        hf = j // sub
        rows = pl.ds(j % sub, n1, stride=sub)
        xj = jnp.concatenate([xs[hf, c, rows, :] for c in range(n_ct)], axis=-1)
        a = _dot(f, xj.astype(BF16))
        ar, ai = a[:n1], a[n1:]
        tc, ts = twc_ref[j], tws_ref[j]
        br = ar * tc + ai * ts
        bi = ai * tc - ar * ts
        for c in range(n_ct):
            brs[hf, c, rows, :] = br[:, c * LANES:(c + 1) * LANES]
            bis[hf, c, rows, :] = bi[:, c * LANES:(c + 1) * LANES]
    for c in range(n_ct):
        cs = slice(c * LANES, (c + 1) * LANES)
        for dst, src in ((br_ref, brs), (bi_ref, bis)):
            halves = [src[hf, c].reshape(n1, sub, LANES) for hf in range(nl2 // sub)]
            dst[:, :, cs] = jnp.concatenate(halves, axis=1).astype(dst.dtype)


def _fourier2_kernel(br_ref, bi_ref, fr_ref, fi_ref, fc_ref, o_ref, os, *, scale):
    n_slab, kb, n2 = br_ref.shape[0], br_ref.shape[1], br_ref.shape[2]
    fr, fi, fc = fr_ref[...], fi_ref[...], fc_ref[...]
    gw = FOURIER_GROUP_W
    sub = 8
    for j in range(kb):
        for sl in range(n_slab):
            b = jnp.concatenate([br_ref[sl, j], bi_ref[sl, j]], axis=0)
            pr = _dot(fr, b).astype(BF16)
            pi = _dot(fi, b).astype(BF16)
            for g in range(SLAB // gw):
                y = _dot(jnp.concatenate([pr[:, g * gw:(g + 1) * gw], pi[:, g * gw:(g + 1) * gw]], axis=-1), fc)
                y = y * scale
                for t in range(gw // LANES):
                    ct = (sl * SLAB + g * gw) // LANES + t
                    os[j // sub, ct, pl.ds(j % sub, n2, stride=sub), :] = y[:, t * LANES:(t + 1) * LANES]
    for ct in range(os.shape[1]):
        halves = [os[hf, ct].reshape(n2, sub, LANES) for hf in range(kb // sub)]
        o_ref[:, :, ct * LANES:(ct + 1) * LANES] = jnp.concatenate(halves, axis=1).astype(o_ref.dtype)


def _fourier(proj):
    n_seg, m, _ = proj.shape
    n1 = 1 << ((m.bit_length() - 1) // 2)
    n2 = m // n1
    assert n1 * n2 == m and n1 % 16 == 0 and n2 % 16 == 0
    n_slab = 2
    width = n_slab * SLAB
    c1, s1 = _dft_tables(n1)
    f1 = jnp.asarray(np.concatenate([c1, -s1], axis=0), BF16)
    tau = 2.0 * np.pi * np.outer(np.arange(n2), np.arange(n1)) / m
    twc = jnp.asarray(np.cos(tau)[:, :, None], F32)
    tws = jnp.asarray(np.sin(tau)[:, :, None], F32)
    nl2 = 16
    b_spec = pl.BlockSpec((None, n1, nl2, SLAB), lambda s, j: (s, 0, j, 0))
    b_shape = jax.ShapeDtypeStruct((n_slab, n1, n2, SLAB), BF16)
    tw_spec = pl.BlockSpec((nl2, n1, 1), lambda s, j: (j, 0, 0))
    br, bi = pl.pallas_call(
        _fourier1_kernel, grid=(n_slab, n2 // nl2),
        in_specs=[pl.BlockSpec((None, n1, nl2, SLAB), lambda s, j: (SLAB_F + s, 0, j, 0)),
                  pl.BlockSpec((2 * n1, n1), lambda s, j: (0, 0)), tw_spec, tw_spec],
        out_specs=[b_spec, b_spec], out_shape=[b_shape, b_shape],
        scratch_shapes=[pltpu.VMEM((nl2 // 8, SLAB // LANES, n1 * 8, LANES), F32)] * 3,
        compiler_params=_cp("parallel", "arbitrary"), name="fourier1",
    )(proj.reshape(n_seg, n1, n2, SLAB), f1, twc, tws)

    c2, s2 = _dft_tables(n2)
    cc, sc = _dft_tables(FOURIER_GROUP_W)
    kb = 16
    in4 = pl.BlockSpec((n_slab, kb, n2, SLAB), lambda k: (0, k, 0, 0))
    full = lambda a: pl.BlockSpec(a.shape, lambda k: (0, 0))
    fr = jnp.asarray(np.concatenate([c2, s2], axis=1), BF16)
    fi = jnp.asarray(np.concatenate([-s2, c2], axis=1), BF16)
    fc = jnp.asarray(np.concatenate([cc, sc], axis=0), BF16)
    out = pl.pallas_call(
        functools.partial(_fourier2_kernel, scale=1.0 / math.sqrt(m * FOURIER_GROUP_W)),
        grid=(n1 // kb,),
        in_specs=[in4, in4, full(fr), full(fi), full(fc)],
        out_specs=pl.BlockSpec((n2, kb, width), lambda k: (0, k, 0)),
        out_shape=jax.ShapeDtypeStruct((n2, n1, width), BF16),
        scratch_shapes=[pltpu.VMEM((kb // 8, width // LANES, n2 * 8, LANES), F32)],
        compiler_params=_cp("parallel"), name="fourier2",
    )(br, bi, fr, fi, fc)
    return out.reshape(m, width)


def _fourier_direct_kernel(u0_ref, u1_ref, cl_ref, sl_ref, cc_ref, sc_ref, o_ref, *, scale):
    gw = FOURIER_GROUP_W
    for sl, u_ref in enumerate((u0_ref, u1_ref)):
        u = u_ref[...]
        pr = _dot(cl_ref[...], u).astype(BF16)
        pi = (-_dot(sl_ref[...], u)).astype(BF16)
        for g in range(SLAB // gw):
            y = _dot(pr[:, g * gw:(g + 1) * gw], cc_ref[...]) + _dot(pi[:, g * gw:(g + 1) * gw], sc_ref[...])
            c0 = sl * SLAB + g * gw
            o_ref[:, c0:c0 + gw] = (y * scale).astype(o_ref.dtype)


def _fourier_direct(proj):
    m = proj.shape[1]
    cl, sl = _dft_tables(m)
    cc, sc = _dft_tables(FOURIER_GROUP_W)
    full = lambda n: pl.BlockSpec((n, n), lambda i: (0, 0))
    return pl.pallas_call(
        functools.partial(_fourier_direct_kernel, scale=1.0 / math.sqrt(m * FOURIER_GROUP_W)),
        grid=(1,),
        in_specs=[pl.BlockSpec((None, m, SLAB), lambda i: (SLAB_F, 0, 0)),
                  pl.BlockSpec((None, m, SLAB), lambda i: (SLAB_F + 1, 0, 0)),
                  full(m), full(m), full(FOURIER_GROUP_W), full(FOURIER_GROUP_W)],
        out_specs=pl.BlockSpec((m, 2 * SLAB), lambda i: (0, 0)),
        out_shape=jax.ShapeDtypeStruct((m, 2 * SLAB), BF16),
        compiler_params=_cp("arbitrary"), name="fourier_direct",
    )(proj, proj, jnp.asarray(cl, BF16), jnp.asarray(sl, BF16), jnp.asarray(cc, BF16), jnp.asarray(sc, BF16))


CONF_HALO = 16
CONF_ROWS = 32


def _conformer_kernel(*refs, n_t):
    cur = refs[0:4]
    prv = refs[4:8]
    nxt = refs[8:12]
    cw_ref, cb_ref, lg_ref, lb_ref, o_ref, ext, shifted, conv = refs[12:]
    i = pl.program_id(0)
    t = cur[0].shape[0]
    width = 2 * SLAB

    def glu(parts, keep):
        for sl in range(2):
            a = parts[sl][...].astype(F32)
            g = parts[2 + sl][...].astype(F32)
            yield sl, jnp.where(keep, a * _sigmoid(g), 0.0)

    for sl, v in glu(prv, i > 0):
        ext[0:CONF_HALO, sl * SLAB:(sl + 1) * SLAB] = v
    for sl, v in glu(cur, True):
        ext[CONF_HALO:CONF_HALO + t, sl * SLAB:(sl + 1) * SLAB] = v
    for sl, v in glu(nxt, i < n_t - 1):
        ext[CONF_HALO + t:, sl * SLAB:(sl + 1) * SLAB] = v
    rows = shifted.shape[1]
    for j in range(8):
        shifted[j] = ext[j:j + rows]

    n_ct = width // LANES

    def chunk(rc, _):
        r0 = pl.multiple_of(rc * CONF_ROWS, CONF_ROWS)
        for c in range(n_ct):
            cs = slice(c * LANES, (c + 1) * LANES)
            acc = [jnp.broadcast_to(cb_ref[:, cs], (8, LANES))] * (CONF_ROWS // 8)
            for k in range(CONF_KERNEL):
                off = k + 1
                wk = cw_ref[k, :, cs]
                for r in range(CONF_ROWS // 8):
                    acc[r] = acc[r] + wk * shifted[off % 8, pl.ds(r0 + (off - off % 8) + 8 * r, 8), cs]
            for r in range(CONF_ROWS // 8):
                conv[pl.ds(r0 + 8 * r, 8), cs] = acc[r]
        return 0

    lax.fori_loop(0, t // CONF_ROWS, chunk, 0)
    v = conv[...]
    d = v - jnp.mean(v, axis=-1, keepdims=True)
    y = d * lax.rsqrt(jnp.mean(d * d, axis=-1, keepdims=True) + EPS) * lg_ref[...] + lb_ref[...]
    o_ref[...] = (y * _sigmoid(y)).astype(o_ref.dtype)


def _conformer(proj, layer, conv_w_rep, conv_b, ln_g, ln_b):
    m = proj.shape[1]
    width = conv_w_rep.shape[-1]
    t = min(256, m)
    n_t = m // t
    hb = CONF_HALO
    cur = lambda s: pl.BlockSpec((None, t, SLAB), lambda i: (s, i, 0))
    prv = lambda s: pl.BlockSpec((None, hb, SLAB), lambda i: (s, jnp.maximum(i * (t // hb) - 1, 0), 0))
    nxt = lambda s: pl.BlockSpec((None, hb, SLAB), lambda i: (s, jnp.minimum((i + 1) * (t // hb), m // hb - 1), 0))
    slabs = (SLAB_CA, SLAB_CA + 1, SLAB_CG, SLAB_CG + 1)
    vec = lambda a: a.reshape(a.shape[0], 1, width)
    vspec = pl.BlockSpec((None, 1, width), lambda i: (layer, 0, 0))
    return pl.pallas_call(
        functools.partial(_conformer_kernel, n_t=n_t), grid=(n_t,),
        in_specs=[cur(s) for s in slabs] + [prv(s) for s in slabs] + [nxt(s) for s in slabs]
        + [pl.BlockSpec((None, CONF_KERNEL, 8, width), lambda i: (layer, 0, 0, 0)), vspec, vspec, vspec],
        out_specs=pl.BlockSpec((t, width), lambda i: (i, 0)),
        out_shape=jax.ShapeDtypeStruct((m, width), BF16),
        scratch_shapes=[pltpu.VMEM((t + 2 * hb, width), F32), pltpu.VMEM((8, t + 2 * hb - 8, width), F32),
                        pltpu.VMEM((t, width), F32)],
        compiler_params=_cp("parallel"), name="conformer",
    )(*([proj] * 12), conv_w_rep, vec(conv_b), vec(ln_g), vec(ln_b))


def _merge_kernel(h_ref, gw_ref, gb_ref, ya_ref, yb_ref, yc_ref, yd_ref, wb_ref, o_ref):
    gw = GATE_BLOCK_W
    acc = None
    for b, y_ref in enumerate((ya_ref, yb_ref, yc_ref, yd_ref)):
        pre = jnp.concatenate([_dot(h_ref[:, n * gw:(n + 1) * gw], gw_ref[b, n])
                               for n in range(gw_ref.shape[1])], axis=-1)
        term = _sigmoid(pre + gb_ref[b:b + 1, :]) * _dot(y_ref[...], wb_ref[b])
        acc = term if acc is None else acc + term
    o_ref[...] = acc.astype(o_ref.dtype)


def _merge(h, ys, gate_w, gate_b, w_branch, layer):
    m, d = h.shape
    bw = ys[0].shape[-1]
    nbr = len(ys)
    tm = min(1024, m)
    gpb = 2
    tn = gpb * GATE_BLOCK_W
    y_spec = pl.BlockSpec((tm, bw), lambda i, n: (i, 0))
    return pl.pallas_call(
        _merge_kernel, grid=(m // tm, d // tn),
        in_specs=[pl.BlockSpec((tm, tn), lambda i, n: (i, n)),
                  pl.BlockSpec((None, nbr, gpb, GATE_BLOCK_W, GATE_BLOCK_W), lambda i, n: (layer, 0, n, 0, 0)),
                  pl.BlockSpec((None, nbr, tn), lambda i, n: (layer, 0, n)),
                  y_spec, y_spec, y_spec, y_spec,
                  pl.BlockSpec((None, nbr, bw, tn), lambda i, n: (layer, 0, 0, n))],
        out_specs=pl.BlockSpec((tm, tn), lambda i, n: (i, n)),
        out_shape=jax.ShapeDtypeStruct((m, d), BF16),
        compiler_params=_cp("parallel", "arbitrary"), name="merge",
    )(h, gate_w, gate_b, *ys, w_branch)


def kernel(x, c, ctx, c_ctx, ada_w, ada_b, norm_mix_g, norm_ffn_g, w_in, qk_norm_g, attn_sink,
           lru_conv_w, lru_conv_b, lru_lambda, lru_wa, lru_ba, lru_wx, lru_bx,
           conf_conv_w, conf_conv_b, conf_ln_g, conf_ln_b, w_branch, gate_w, gate_b, w_out,
           ffn_w_gate, ffn_w_up, ffn_w_down, router_w, router_b, moe_w_gate, moe_w_up, moe_w_down):
    assert x.shape[0] == 1 and ctx.shape[0] == 1 and c.shape[0] == 1
    depth = ada_w.shape[0]
    seq, d = x.shape[1], x.shape[2]
    cx = ctx.shape[1]
    xl = x[0]
    xc = ctx[0]

    w_in_b, w_branch_b, gate_w_b, w_out_b = (a.astype(BF16) for a in (w_in, w_branch, gate_w, w_out))
    ffn_g_b, ffn_u_b, ffn_d_b = (a.astype(BF16) for a in (ffn_w_gate, ffn_w_up, ffn_w_down))
    moe_gu_b = jnp.concatenate([moe_w_gate.astype(BF16), moe_w_up.astype(BF16)], axis=-1)
    moe_d_b = moe_w_down.astype(BF16).reshape(moe_w_down.shape[0], -1, d)
    lru_wa_b, lru_wx_b = lru_wa.astype(BF16), lru_wx.astype(BF16)
    conf_w_rep = jnp.broadcast_to(conf_conv_w[:, :, None, :], conf_conv_w.shape[:2] + (8,) + conf_conv_w.shape[2:])
    router_w_p = jnp.pad(router_w, ((0, 0), (0, 0), (0, LANES - N_EXPERTS)))
    router_b_p = jnp.pad(router_b, ((0, 0), (0, LANES - N_EXPERTS)))[:, None, :]

    c_rows = jnp.concatenate([c, c_ctx[None, :], jnp.zeros((6, d), F32)], axis=0)
    mod = _adaln(c_rows, ada_w, ada_b)

    cos_l, sin_l = _rope_tables(seq)
    cos_c, sin_c = jnp.ones((cx, HEAD_DIM), F32), jnp.zeros((cx, HEAD_DIM), F32)
    zero_state = jnp.zeros((8, lru_conv_w.shape[-1]), F32)
    lru_p = (lru_conv_w, lru_conv_b, lru_lambda, lru_wa_b, lru_ba, lru_wx_b, lru_bx)

    def channel_mixer(xs, l, row):
        li = l // 2
        if l % 2 == 0:
            h = _norm(xs, norm_ffn_g, mod, l, 3, row)
            hid = _ffn_up(h, ffn_g_b, ffn_u_b, li)
            return _mm_res(hid, ffn_d_b, li, xs, mod, l, 5, row)
        h, gates = _norm(xs, norm_ffn_g, mod, l, 3, row, router=(router_w_p, router_b_p, li))
        hid = _moe_up(h, gates, moe_gu_b, li)
        return _mm_res(hid, moe_d_b, li, xs, mod, l, 5, row)

    for l in range(depth):
        need_ctx = l < depth - 1
        h_l = _norm(xl, norm_mix_g, mod, l, 0, 0)
        h_c = _norm(xc, norm_mix_g, mod, l, 0, 1)
        p_l = _in_proj(h_l, w_in_b, l)
        p_c = _in_proj(h_c, w_in_b, l)
        qn_l, kn_l = _qk_prep(p_l, cos_l, sin_l, qk_norm_g, l)
        qn_c, kn_c = _qk_prep(p_c, cos_c, sin_c, qk_norm_g, l)
        sink = attn_sink[l]
        ya_l = _attention(qn_l, kn_l, p_l, kn_c, p_c, sink)
        hf_c, fin_f = _lru(p_c, l, 0, zero_state, *lru_p)
        yb_c, fin_b = _lru(p_c, l, 1, zero_state, *lru_p, h_first=hf_c if need_ctx else None)
        hf_l, _ = _lru(p_l, l, 0, fin_f, *lru_p)
        yb_l, _ = _lru(p_l, l, 1, fin_b, *lru_p, h_first=hf_l)
        yc_l = _fourier(p_l)
        yd_l = _conformer(p_l, l, conf_w_rep, conf_conv_b, conf_ln_g, conf_ln_b)
        acc_l = _merge(h_l, (ya_l, yb_l, yc_l, yd_l), gate_w_b, gate_b, w_branch_b, l)
        xl = _mm_res(acc_l, w_out_b, l, xl, mod, l, 2, 0)
        xl = channel_mixer(xl, l, 0)
        if need_ctx:
            ya_c = _ctx_attention(qn_c, kn_c, p_c, sink)
            yc_c = _fourier_direct(p_c)
            yd_c = _conformer(p_c, l, conf_w_rep, conf_conv_b, conf_ln_g, conf_ln_b)
            acc_c = _merge(h_c, (ya_c, yb_c, yc_c, yd_c), gate_w_b, gate_b, w_branch_b, l)
            xc = _mm_res(acc_c, w_out_b, l, xc, mod, l, 2, 1)
            xc = channel_mixer(xc, l, 1)
    return xl[None]
```
